```python
import math, functools
import jax, jax.numpy as jnp
from jax import lax
import numpy as np

D_MODEL = 2048
BATCH = 8
SEQ = 2048
DEPTH = 1
DEC_BATCH = 128
DEC_SEQ = 8
PAST_LEN = 16384
PAGE_SIZE = 128

NORM_EPS = 1e-6
RWKV_HEAD_DIM = 64
RWKV_WIDTH = D_MODEL // 2
RWKV_HEADS = RWKV_WIDTH // RWKV_HEAD_DIM
W_LORA = 96
A_LORA = 96
G_LORA = 256
RWKV_PROJ = 3 * RWKV_WIDTH + W_LORA + A_LORA + G_LORA
RWKV_GN_EPS = 64e-5
V_DIM = 128
MLA_WIDTH = D_MODEL - RWKV_WIDTH
MLA_HEADS = MLA_WIDTH // V_DIM
Q_LORA = 512
KV_LORA = 256
NOPE_DIM = 128
ROPE_DIM = 64
ROPE_THETA = 10000.0
MLA_SCALE = (NOPE_DIM + ROPE_DIM) ** -0.5
Q_BLOCK = 128
IN_PROJ = RWKV_PROJ + Q_LORA + KV_LORA + ROPE_DIM
MIX_WIDTH = RWKV_WIDTH + MLA_WIDTH
MEM_TOKENS = 256
MEM_HEADS = 4
MEM_HEAD_DIM = 128
MEM_WIDTH = MEM_HEADS * MEM_HEAD_DIM
MEM_SCALE = MEM_HEAD_DIM ** -0.5
N_GROUPS = 4
EXPERTS_PER_GROUP = 8
N_EXPERTS = N_GROUPS * EXPERTS_PER_GROUP
TOP_K = 2
EXPERT_FF = 512
MOE_BLOCK = 128

kernel_name = "hymba_rwkv7_mla_hmoe_step"

F32 = jnp.float32


def rmsnorm(x, g):
    xf = x.astype(F32)
    y = xf * lax.rsqrt(jnp.mean(xf * xf, axis=-1, keepdims=True) + NORM_EPS)
    return (y * g.astype(F32)).astype(x.dtype)


def rope_tables(pos):
    half = ROPE_DIM // 2
    inv = ROPE_THETA ** (-jnp.arange(half, dtype=F32) / half)
    ang = pos.astype(F32)[:, None] * inv[None, :]
    return jnp.cos(ang), jnp.sin(ang)


def apply_rope(x, cos, sin):
    half = ROPE_DIM // 2
    xf = x.astype(F32)
    x1, x2 = xf[..., :half], xf[..., half:]
    return jnp.concatenate([x1 * cos - x2 * sin, x1 * sin + x2 * cos], axis=-1).astype(x.dtype)


def rwkv7_mix(p, prev, state0, mu, w0, w_w2, a0, w_a2, w_g2, k_k, k_a, r_k, gn_g, gn_b):
    B, S, _ = p.shape
    H, N = RWKV_HEADS, RWKV_HEAD_DIM
    shifted = jnp.concatenate([prev[:, None, :].astype(p.dtype), p[:, :-1]], axis=1)
    ps = p + (shifted - p) * mu
    i1, i2, i3 = RWKV_WIDTH, 2 * RWKV_WIDTH, 3 * RWKV_WIDTH
    i4 = i3 + W_LORA
    i5 = i4 + A_LORA
    r, k, v = ps[..., :i1], ps[..., i1:i2], ps[..., i2:i3]
    xw, xa, xg = ps[..., i3:i4], ps[..., i4:i5], ps[..., i5:]
    w_log = -jax.nn.softplus(-(w0 + jnp.tanh(xw) @ w_w2).astype(F32)) - 0.5
    hs = (B, S, H, N)
    decay = jnp.exp(-jnp.exp(w_log)).reshape(hs)
    a = jax.nn.sigmoid((a0 + xa @ w_a2).astype(F32)).reshape(hs)
    g = jax.nn.sigmoid(xg) @ w_g2
    rf = r.astype(F32).reshape(hs)
    kf = k.astype(F32).reshape(hs)
    vf = v.astype(F32).reshape(hs)
    kk = kf * k_k.astype(F32).reshape(H, N)
    kk = kk / jnp.maximum(jnp.sqrt(jnp.sum(kk * kk, axis=-1, keepdims=True)), 1e-12)
    kf = kf * (1.0 + (a - 1.0) * k_a.astype(F32).reshape(H, N))

    def step(st, inp):
        r_t, w_t, k_t, v_t, kk_t, b_t = inp
        sa = jnp.einsum('bhvk,bhk->bhv', st, -kk_t)
        st = st * w_t[:, :, None, :] + sa[..., None] * b_t[:, :, None, :] + v_t[..., None] * k_t[:, :, None, :]
        return st, jnp.einsum('bhvk,bhk->bhv', st, r_t)

    xs = (jnp.swapaxes(rf, 0, 1), jnp.swapaxes(decay, 0, 1), jnp.swapaxes(kf, 0, 1),
          jnp.swapaxes(vf, 0, 1), jnp.swapaxes(kk, 0, 1), jnp.swapaxes(kk * a, 0, 1))
    st_final, ys = lax.scan(step, state0.astype(F32), xs)
    y = jnp.swapaxes(ys, 0, 1)
    mean = jnp.mean(y, axis=-1, keepdims=True)
    var = jnp.mean(jnp.square(y - mean), axis=-1, keepdims=True)
    y = ((y - mean) * lax.rsqrt(var + RWKV_GN_EPS)).reshape(B, S, RWKV_WIDTH)
    y = y * gn_g.astype(F32) + gn_b.astype(F32)
    bonus = jnp.sum(rf * kf * r_k.astype(F32), axis=-1, keepdims=True) * vf
    out = ((y + bonus.reshape(B, S, RWKV_WIDTH)) * g.astype(F32)).astype(p.dtype)
    return out, st_final.astype(state0.dtype), p[:, -1].astype(prev.dtype)


def mla_project(q_a, kv_a, cos, sin, q_norm_g, w_qb, kv_norm_g, w_kvb):
    B, S, _ = q_a.shape
    q = (rmsnorm(q_a, q_norm_g) @ w_qb).reshape(B, S, MLA_HEADS, NOPE_DIM + ROPE_DIM)
    q_nope = q[..., :NOPE_DIM]
    q_pe = apply_rope(q[..., NOPE_DIM:], cos[:, None, :], sin[:, None, :])
    c_kv = rmsnorm(kv_a[..., :KV_LORA], kv_norm_g)
    k_pe = apply_rope(kv_a[..., KV_LORA:], cos, sin)
    w_uk = w_kvb.reshape(KV_LORA, MLA_HEADS, NOPE_DIM + V_DIM)[..., :NOPE_DIM]
    q_lat = jnp.einsum('bshd,chd->bshc', q_nope, w_uk)
    return q_lat, q_pe, c_kv, k_pe


def mla_attend_prompt(q_lat, q_pe, c_kv, k_pe):
    B, S, H, C = q_lat.shape
    nb = S // Q_BLOCK
    ql = jnp.swapaxes(q_lat.reshape(B, nb, Q_BLOCK, H, C), 0, 1)
    qp = jnp.swapaxes(q_pe.reshape(B, nb, Q_BLOCK, H, ROPE_DIM), 0, 1)
    kpos = jnp.arange(S)

    def one_block(args):
        blk, ql_b, qp_b = args
        s = (jnp.einsum('bqhc,bkc->bhqk', ql_b, c_kv) + jnp.einsum('bqhr,bkr->bhqk', qp_b, k_pe)).astype(F32) * MLA_SCALE
        qpos = blk * Q_BLOCK + jnp.arange(Q_BLOCK)
        s = jnp.where(kpos[None, :] <= qpos[:, None], s, -jnp.inf)
        pr = jax.nn.softmax(s, axis=-1).astype(c_kv.dtype)
        return jnp.einsum('bhqk,bkc->bqhc', pr, c_kv)

    o = lax.map(one_block, (jnp.arange(nb), ql, qp))
    return jnp.swapaxes(o, 0, 1).reshape(B, S, H, C)


def mla_attend_sample(q_lat, q_pe, c_kv, k_pe, past_c, past_kpe):
    T = q_lat.shape[1]
    P = past_c.shape[1]
    s_past = (jnp.einsum('bqhc,bkc->bhqk', q_lat, past_c) + jnp.einsum('bqhr,bkr->bhqk', q_pe, past_kpe)).astype(F32) * MLA_SCALE
    s_new = (jnp.einsum('bqhc,bkc->bhqk', q_lat, c_kv) + jnp.einsum('bqhr,bkr->bhqk', q_pe, k_pe)).astype(F32) * MLA_SCALE
    causal = jnp.tril(jnp.ones((T, T), dtype=bool))
    s_new = jnp.where(causal, s_new, -jnp.inf)
    pr = jax.nn.softmax(jnp.concatenate([s_past, s_new], axis=-1), axis=-1).astype(c_kv.dtype)
    return jnp.einsum('bhqk,bkc->bqhc', pr[..., :P], past_c) + jnp.einsum('bhqk,bkc->bqhc', pr[..., P:], c_kv)


def memory_kv(mem, mem_norm_g, w_mk, w_mv):
    B, M, _ = mem.shape
    m = rmsnorm(mem, mem_norm_g)
    return ((m @ w_mk).reshape(B, M, MEM_HEADS, MEM_HEAD_DIM),
            (m @ w_mv).reshape(B, M, MEM_HEADS, MEM_HEAD_DIM))


def memory_attend(h, mem_k, mem_v, w_mq, w_mo):
    B, S, _ = h.shape
    q = (h @ w_mq).reshape(B, S, MEM_HEADS, MEM_HEAD_DIM)
    s = jnp.einsum('bqhd,bkhd->bhqk', q, mem_k).astype(F32) * MEM_SCALE
    pr = jax.nn.softmax(s, axis=-1).astype(mem_v.dtype)
    o = jnp.einsum('bhqk,bkhd->bqhd', pr, mem_v).reshape(B, S, MEM_WIDTH)
    return o @ w_mo


def moe_dispatch(xt, eid, wts, w_gate, w_up, w_down):
    N, D = xt.shape
    A = N * TOP_K
    flat_e = eid.reshape(-1).astype(jnp.int32)
    flat_tok = jnp.repeat(jnp.arange(N, dtype=jnp.int32), TOP_K)
    flat_w = wts.reshape(-1)
    order = jnp.argsort(flat_e)
    se, stok, sw = flat_e[order], flat_tok[order], flat_w[order]
    counts = jnp.bincount(flat_e, length=N_EXPERTS).astype(jnp.int32)
    starts = jnp.cumsum(counts) - counts
    padded = (counts + MOE_BLOCK - 1) // MOE_BLOCK * MOE_BLOCK
    pends = jnp.cumsum(padded)
    pstarts = pends - padded
    dest = pstarts[se] + (jnp.arange(A, dtype=jnp.int32) - starts[se])
    n_blocks = -(-A // MOE_BLOCK) + N_EXPERTS
    n_slots = n_blocks * MOE_BLOCK
    slot_tok = jnp.full((n_slots,), N, dtype=jnp.int32).at[dest].set(stok)
    slot_w = jnp.zeros((n_slots,), F32).at[dest].set(sw)
    block_start = jnp.arange(n_blocks, dtype=jnp.int32) * MOE_BLOCK
    block_e = jnp.minimum(jnp.searchsorted(pends, block_start, side='right'), N_EXPERTS - 1)
    x_pad = jnp.concatenate([xt, jnp.zeros((1, D), xt.dtype)], axis=0)
    xb = x_pad[slot_tok].reshape(n_blocks, MOE_BLOCK, D)

    def expert_block(args):
        xblk, e = args
        hid = jax.nn.silu(xblk @ w_gate[e]) * (xblk @ w_up[e])
        return hid @ w_down[e]

    yb = lax.map(expert_block, (xb, block_e)).reshape(n_slots, D)
    y = jnp.zeros((N + 1, D), F32).at[slot_tok].add(yb.astype(F32) * slot_w[:, None])
    return y[:N].astype(xt.dtype)


def hmoe(h, w_router_c, b_router_c, w_router_f, b_router_f, w_gate, w_up, w_down):
    B, S, D = h.shape
    xt = h.reshape(B * S, D)
    lc = (xt @ w_router_c).astype(F32) + b_router_c.astype(F32)
    pc = jax.nn.softmax(lc, axis=-1)
    grp = jnp.argmax(lc, axis=-1)
    gate_c = jnp.take_along_axis(pc, grp[:, None], axis=-1)
    lf = ((xt @ w_router_f).astype(F32) + b_router_f.astype(F32)).reshape(-1, N_GROUPS, EXPERTS_PER_GROUP)
    lf_g = jnp.take_along_axis(lf, grp[:, None, None], axis=1)[:, 0]
    pf = jax.nn.softmax(lf_g, axis=-1)
    topv, topi = lax.top_k(pf, TOP_K)
    wts = gate_c * topv / jnp.sum(topv, axis=-1, keepdims=True)
    eid = grp[:, None] * EXPERTS_PER_GROUP + topi
    return moe_dispatch(xt, eid, wts, w_gate, w_up, w_down).reshape(B, S, D)


def decoder_layer(x, cos, sin, rwkv_prev, rwkv_state0, attend, mem_k, mem_v, lp):
    B, S, _ = x.shape
    h = rmsnorm(x, lp['ln_mix_g'])
    proj = h @ lp['w_in']
    p_rwkv = proj[..., :RWKV_PROJ]
    q_a = proj[..., RWKV_PROJ:RWKV_PROJ + Q_LORA]
    kv_a = proj[..., RWKV_PROJ + Q_LORA:]
    o_rwkv, rwkv_state, rwkv_shift = rwkv7_mix(
        p_rwkv, rwkv_prev, rwkv_state0, lp['rwkv_mu'], lp['rwkv_w0'], lp['rwkv_w_w2'], lp['rwkv_a0'],
        lp['rwkv_w_a2'], lp['rwkv_w_g2'], lp['rwkv_k_k'], lp['rwkv_k_a'], lp['rwkv_r_k'], lp['rwkv_gn_g'], lp['rwkv_gn_b'])
    q_lat, q_pe, c_kv, k_pe = mla_project(q_a, kv_a, cos, sin, lp['mla_q_norm_g'], lp['mla_w_qb'],
                                          lp['mla_kv_norm_g'], lp['mla_w_kvb'])
    o_lat = attend(q_lat, q_pe, c_kv, k_pe)
    w_uv = lp['mla_w_kvb'].reshape(KV_LORA, MLA_HEADS, NOPE_DIM + V_DIM)[..., NOPE_DIM:]
    o_mla = jnp.einsum('bshc,chv->bshv', o_lat, w_uv).reshape(B, S, MLA_WIDTH)
    x = x + jnp.concatenate([o_rwkv, o_mla], axis=-1) @ lp['w_out']
    x = x + memory_attend(rmsnorm(x, lp['ln_mem_g']), mem_k, mem_v, lp['w_mq'], lp['w_mo'])
    x = x + hmoe(rmsnorm(x, lp['ln_moe_g']), lp['w_router_c'], lp['b_router_c'], lp['w_router_f'],
                 lp['b_router_f'], lp['w_gate'], lp['w_up'], lp['w_down'])
    return x, c_kv, k_pe, rwkv_state, rwkv_shift


def setup_inputs(seed: int = 0) -> dict:
    key = jax.random.key(seed)
    ks = iter(jax.random.split(key, 64))
    L = DEPTH
    n_pages = PAST_LEN // PAGE_SIZE
    n_used = DEC_BATCH * n_pages
    n_phys = n_used + n_used // 4

    def nrm(shape, scale):
        return jax.random.normal(next(ks), shape, F32) * scale

    def gain(shape):
        return 1.0 + nrm(shape, 0.02)

    def unif(shape, lo, hi):
        return jax.random.uniform(next(ks), shape, F32, lo, hi)

    return {
        'x_prompt': nrm((BATCH, SEQ, D_MODEL), 1.0),
        'x_sample': nrm((DEC_BATCH, DEC_SEQ, D_MODEL), 1.0),
        'mem_prompt': nrm((BATCH, MEM_TOKENS, D_MODEL), 1.0),
        'cache_kv_latent': nrm((L, n_phys, PAGE_SIZE, KV_LORA), 1.0),
        'cache_k_rope': nrm((L, n_phys, PAGE_SIZE, ROPE_DIM), 1.0),
        'cache_mem_k': nrm((L, DEC_BATCH, MEM_TOKENS, MEM_HEADS, MEM_HEAD_DIM), 1.0),
        'cache_mem_v': nrm((L, DEC_BATCH, MEM_TOKENS, MEM_HEADS, MEM_HEAD_DIM), 1.0),
        'state_rwkv': nrm((L, DEC_BATCH, RWKV_HEADS, RWKV_HEAD_DIM, RWKV_HEAD_DIM), 0.5),
        'state_rwkv_shift': nrm((L, DEC_BATCH, RWKV_PROJ), 1.0),
        'page_table': jax.random.permutation(next(ks), n_phys)[:n_used].reshape(DEC_BATCH, n_pages).astype(jnp.int32),
        'ln_mix_g': gain((L, D_MODEL)),
        'w_in': nrm((L, D_MODEL, IN_PROJ), D_MODEL ** -0.5),
        'rwkv_mu': unif((L, RWKV_PROJ), 0.0, 1.0),
        'rwkv_w0': unif((L, RWKV_WIDTH), -6.0, 1.0),
        'rwkv_w_w2': nrm((L, W_LORA, RWKV_WIDTH), 0.5 * W_LORA ** -0.5),
        'rwkv_a0': nrm((L, RWKV_WIDTH), 0.5),
        'rwkv_w_a2': nrm((L, A_LORA, RWKV_WIDTH), 0.5 * A_LORA ** -0.5),
        'rwkv_w_g2': nrm((L, G_LORA, RWKV_WIDTH), G_LORA ** -0.5),
        'rwkv_k_k': 0.85 + nrm((L, RWKV_WIDTH), 0.05),
        'rwkv_k_a': gain((L, RWKV_WIDTH)),
        'rwkv_r_k': nrm((L, RWKV_HEADS, RWKV_HEAD_DIM), 0.1),
        'rwkv_gn_g': gain((L, RWKV_WIDTH)),
        'rwkv_gn_b': nrm((L, RWKV_WIDTH), 0.02),
        'mla_q_norm_g': gain((L, Q_LORA)),
        'mla_w_qb': nrm((L, Q_LORA, MLA_HEADS * (NOPE_DIM + ROPE_DIM)), Q_LORA ** -0.5),
        'mla_kv_norm_g': gain((L, KV_LORA)),
        'mla_w_kvb': nrm((L, KV_LORA, MLA_HEADS * (NOPE_DIM + V_DIM)), KV_LORA ** -0.5),
        'w_out': nrm((L, MIX_WIDTH, D_MODEL), MIX_WIDTH ** -0.5),
        'ln_mem_g': gain((L, D_MODEL)),
        'mem_norm_g': gain((L, D_MODEL)),
        'w_mq': nrm((L, D_MODEL, MEM_WIDTH), D_MODEL ** -0.5),
        'w_mk': nrm((L, D_MODEL, MEM_WIDTH), D_MODEL ** -0.5),
        'w_mv': nrm((L, D_MODEL, MEM_WIDTH), D_MODEL ** -0.5),
        'w_mo': nrm((L, MEM_WIDTH, D_MODEL), MEM_WIDTH ** -0.5),
        'ln_moe_g': gain((L, D_MODEL)),
        'w_router_c': nrm((L, D_MODEL, N_GROUPS), D_MODEL ** -0.5),
        'b_router_c': nrm((L, N_GROUPS), 0.01),
        'w_router_f': nrm((L, D_MODEL, N_EXPERTS), D_MODEL ** -0.5),
        'b_router_f': nrm((L, N_EXPERTS), 0.01),
        'w_gate': nrm((L, N_EXPERTS, D_MODEL, EXPERT_FF), D_MODEL ** -0.5),
        'w_up': nrm((L, N_EXPERTS, D_MODEL, EXPERT_FF), D_MODEL ** -0.5),
        'w_down': nrm((L, N_EXPERTS, EXPERT_FF, D_MODEL), EXPERT_FF ** -0.5),
        'final_norm_g': gain((D_MODEL,)),
    }


def reference(x_prompt, x_sample, mem_prompt, cache_kv_latent, cache_k_rope, cache_mem_k, cache_mem_v,
              state_rwkv, state_rwkv_shift, page_table, ln_mix_g, w_in, rwkv_mu, rwkv_w0, rwkv_w_w2, rwkv_a0,
              rwkv_w_a2, rwkv_w_g2, rwkv_k_k, rwkv_k_a, rwkv_r_k, rwkv_gn_g, rwkv_gn_b, mla_q_norm_g, mla_w_qb,
              mla_kv_norm_g, mla_w_kvb, w_out, ln_mem_g, mem_norm_g, w_mq, w_mk, w_mv, w_mo, ln_moe_g,
              w_router_c, b_router_c, w_router_f, b_router_f, w_gate, w_up, w_down, final_norm_g):
    B, S, _ = x_prompt.shape
    DB, T, _ = x_sample.shape
    past_len = page_table.shape[1] * PAGE_SIZE
    cos_p, sin_p = rope_tables(jnp.arange(S))
    cos_s, sin_s = rope_tables(past_len + jnp.arange(T))
    xp, xs = x_prompt, x_sample
    kvp, kpp, mkp, mvp, stp, shp = [], [], [], [], [], []
    kvs, kps, sts, shs = [], [], [], []
    for l in range(DEPTH):
        lp = {
            'ln_mix_g': ln_mix_g[l], 'w_in': w_in[l], 'rwkv_mu': rwkv_mu[l], 'rwkv_w0': rwkv_w0[l],
            'rwkv_w_w2': rwkv_w_w2[l], 'rwkv_a0': rwkv_a0[l], 'rwkv_w_a2': rwkv_w_a2[l], 'rwkv_w_g2': rwkv_w_g2[l],
            'rwkv_k_k': rwkv_k_k[l], 'rwkv_k_a': rwkv_k_a[l], 'rwkv_r_k': rwkv_r_k[l], 'rwkv_gn_g': rwkv_gn_g[l],
            'rwkv_gn_b': rwkv_gn_b[l], 'mla_q_norm_g': mla_q_norm_g[l], 'mla_w_qb': mla_w_qb[l],
            'mla_kv_norm_g': mla_kv_norm_g[l], 'mla_w_kvb': mla_w_kvb[l], 'w_out': w_out[l],
            'ln_mem_g': ln_mem_g[l], 'w_mq': w_mq[l], 'w_mo': w_mo[l], 'ln_moe_g': ln_moe_g[l],
            'w_router_c': w_router_c[l], 'b_router_c': b_router_c[l], 'w_router_f': w_router_f[l],
            'b_router_f': b_router_f[l], 'w_gate': w_gate[l], 'w_up': w_up[l], 'w_down': w_down[l],
        }
        mem_k, mem_v = memory_kv(mem_prompt, mem_norm_g[l], w_mk[l], w_mv[l])
        prev0 = jnp.zeros((B, RWKV_PROJ), state_rwkv_shift.dtype)
        st0 = jnp.zeros((B, RWKV_HEADS, RWKV_HEAD_DIM, RWKV_HEAD_DIM), state_rwkv.dtype)
        xp, c_p, k_p, st_p, sh_p = decoder_layer(xp, cos_p, sin_p, prev0, st0, mla_attend_prompt, mem_k, mem_v, lp)
        kvp.append(c_p); kpp.append(k_p); mkp.append(mem_k); mvp.append(mem_v); stp.append(st_p); shp.append(sh_p)
        past_c = cache_kv_latent[l, page_table].reshape(DB, past_len, KV_LORA)
        past_kpe = cache_k_rope[l, page_table].reshape(DB, past_len, ROPE_DIM)
        attend_s = functools.partial(mla_attend_sample, past_c=past_c, past_kpe=past_kpe)
        xs, c_s, k_s, st_s, sh_s = decoder_layer(xs, cos_s, sin_s, state_rwkv_shift[l], state_rwkv[l], attend_s,
                                                 cache_mem_k[l], cache_mem_v[l], lp)
        kvs.append(c_s); kps.append(k_s); sts.append(st_s); shs.append(sh_s)
    y_prompt = rmsnorm(xp, final_norm_g)
    y_sample = rmsnorm(xs, final_norm_g)
    return (y_prompt, y_sample, jnp.stack(kvp), jnp.stack(kpp), jnp.stack(mkp), jnp.stack(mvp), jnp.stack(stp),
            jnp.stack(shp), jnp.stack(kvs), jnp.stack(kps), jnp.stack(sts), jnp.stack(shs))
```

```python
import functools

import jax
import jax.numpy as jnp
from jax import lax
from jax.experimental import pallas as pl
from jax.experimental.pallas import tpu as pltpu

F32 = jnp.float32
BF16 = jnp.bfloat16
HIGHEST = lax.Precision.HIGHEST

D_MODEL = 2048
NORM_EPS = 1e-6
HEAD = 64
RWKV_WIDTH = 1024
RWKV_HEADS = 16
W_LORA, A_LORA, G_LORA = 96, 96, 256
LORA_W = W_LORA + A_LORA + G_LORA
LORA_PAD = 512
RWKV_PROJ = 3 * RWKV_WIDTH + LORA_W
GN_EPS = 64e-5
MLA_HEADS = 8
V_DIM = 128
NOPE = 128
ROPE = 64
Q_LORA = 512
KV_LORA = 256
KV_PAD = 512
MLA_SCALE = (NOPE + ROPE) ** -0.5
ROPE_THETA = 10000.0
PAGE = 128
MEM_TOKENS = 256
MEM_HEADS = 4
MEM_HEAD_DIM = 128
MEM_WIDTH = 512
MEM_SCALE = MEM_HEAD_DIM ** -0.5
N_GROUPS = 4
EPG = 8
N_EXPERTS = 32
EXPERT_FF = 512
MOE_BLOCK = 128
ROUTER_PAD = 128

COL_MAIN = 0
COL_QA = 3 * RWKV_WIDTH
COL_LORA = COL_QA + Q_LORA
COL_KV = COL_LORA + LORA_PAD
PROJ_W = COL_KV + KV_PAD

VMEM_LIMIT = 56 * 1024 * 1024


def _cp(sem, vmem=VMEM_LIMIT):
    return pltpu.CompilerParams(dimension_semantics=sem, vmem_limit_bytes=vmem)


def _nt(a, b):
    return lax.dot_general(a, b, (((1,), (1,)), ((), ())), preferred_element_type=F32)


def _tn(a, b):
    return lax.dot_general(a, b, (((0,), (0,)), ((), ())), preferred_element_type=F32)


def _nn(a, b):
    return jnp.dot(a, b, preferred_element_type=F32)


def _mm_kernel(*refs, n_pairs, has_norm, has_res):
    xs = refs[:n_pairs]
    ws = refs[n_pairs:2 * n_pairs]
    idx = 2 * n_pairs
    g_ref = res_ref = None
    if has_norm:
        g_ref = refs[idx]
        idx += 1
    if has_res:
        res_ref = refs[idx]
        idx += 1
    o_ref = refs[idx]
    if has_norm:
        xn_ref = refs[idx + 1]

        @pl.when(pl.program_id(1) == 0)
        def _():
            x = xs[0][...].astype(F32)
            ms = jnp.mean(x * x, axis=-1, keepdims=True)
            xn_ref[...] = ((x * lax.rsqrt(ms + NORM_EPS)) * g_ref[...]).astype(BF16)

        acc = _nn(xn_ref[...], ws[0][...])
    else:
        acc = _nn(xs[0][...].astype(BF16), ws[0][...])
        for x_ref, w_ref in zip(xs[1:], ws[1:]):
            acc = acc + _nn(x_ref[...].astype(BF16), w_ref[...])
    if has_res:
        acc = acc + res_ref[...]
    o_ref[...] = acc.astype(o_ref.dtype)


def _mm(xs, ws, *, g=None, res=None, out_dtype=F32, tm, tn, x_cols=None, name):
    n_pairs = len(xs)
    M = xs[0].shape[0]
    N = ws[0].shape[1]
    x_cols = x_cols or [0] * n_pairs
    assert M % tm == 0 and N % tn == 0
    in_specs, args = [], []
    for x, w, c in zip(xs, ws, x_cols):
        K = w.shape[0]
        in_specs.append(pl.BlockSpec((tm, K), functools.partial(lambda i, j, c: (i, c), c=c)))
        args.append(x)
    for w in ws:
        in_specs.append(pl.BlockSpec((w.shape[0], tn), lambda i, j: (0, j)))
        args.append(w)
    scratch = []
    if g is not None:
        assert n_pairs == 1
        in_specs.append(pl.BlockSpec((1, ws[0].shape[0]), lambda i, j: (0, 0)))
        args.append(g.reshape(1, -1).astype(F32))
        scratch.append(pltpu.VMEM((tm, ws[0].shape[0]), BF16))
    if res is not None:
        in_specs.append(pl.BlockSpec((tm, tn), lambda i, j: (i, j)))
        args.append(res)
    return pl.pallas_call(
        functools.partial(_mm_kernel, n_pairs=n_pairs, has_norm=g is not None, has_res=res is not None),
        grid=(M // tm, N // tn),
        in_specs=in_specs,
        out_specs=pl.BlockSpec((tm, tn), lambda i, j: (i, j)),
        out_shape=jax.ShapeDtypeStruct((M, N), out_dtype),
        scratch_shapes=scratch,
        compiler_params=_cp(("parallel", "arbitrary")),
        name=name,
    )(*args)


def _softplus(z):
    return jnp.maximum(z, 0.0) + jnp.log1p(jnp.exp(-jnp.abs(z)))


def _rwkv_kernel(*refs, C, HB, has_state):
    (r_ref, k_ref, v_ref, l_ref, pr_ref, pk_ref, pv_ref, plo_ref,
     mur_ref, muk_ref, muv_ref, mul_ref,
     w0_ref, a0_ref, kkw_ref, kaw_ref, rkw_ref, gg_ref, gb_ref,
     ww_ref, wa_ref, wg_ref) = refs[:22]
    idx = 22
    s0_ref = None
    if has_state:
        s0_ref = refs[idx]
        idx += 1
    o_ref, st_ref, prow_r, prow_k, prow_v, prow_l = refs[idx:idx + 6]
    c = pl.program_id(2)

    @pl.when(c == 0)
    def _():
        if has_state:
            st_ref[...] = s0_ref[...]
        else:
            st_ref[...] = jnp.zeros(st_ref.shape, F32)
        prow_r[...] = pr_ref[0]
        prow_k[...] = pk_ref[0]
        prow_v[...] = pv_ref[0]
        prow_l[...] = plo_ref[0]

    def shift_mix(x_ref, prow, mu_ref):
        x = x_ref[...]
        sh = pltpu.roll(x, 1, axis=0)
        row = lax.broadcasted_iota(jnp.int32, x.shape, 0)
        sh = jnp.where(row == 0, prow[...], sh)
        prow[...] = x[C - 1:C, :]
        return x + (sh - x) * mu_ref[...]

    r = shift_mix(r_ref, prow_r, mur_ref)
    k = shift_mix(k_ref, prow_k, muk_ref)
    v = shift_mix(v_ref, prow_v, muv_ref)
    lo = shift_mix(l_ref, prow_l, mul_ref)

    lw = _nn(jnp.tanh(lo).astype(BF16), ww_ref[...])
    la = _nn(lo.astype(BF16), wa_ref[...])
    g = _nn(jax.nn.sigmoid(lo).astype(BF16), wg_ref[...])
    w_log = -_softplus(-(w0_ref[...] + lw)) - 0.5
    ld = -jnp.exp(w_log)
    a = jax.nn.sigmoid(a0_ref[...] + la)

    ti = lax.broadcasted_iota(jnp.int32, (C, C), 0)
    tj = lax.broadcasted_iota(jnp.int32, (C, C), 1)
    tri = (ti >= tj).astype(F32)
    cum = jnp.dot(tri, ld, precision=HIGHEST, preferred_element_type=F32)
    eye = (ti == tj).astype(F32)
    strict = ti > tj
    incl = ti >= tj

    n_sq = max(C.bit_length() - 2, 0)
    for h in range(HB):
        sl = slice(h * HEAD, (h + 1) * HEAD)
        rh, kh, vh, ah = r[:, sl], k[:, sl], v[:, sl], a[:, sl]
        ldh, cumh = ld[:, sl], cum[:, sl]
        kk = kh * kkw_ref[:, sl]
        kk = kk / jnp.maximum(jnp.sqrt(jnp.sum(kk * kk, axis=-1, keepdims=True)), 1e-12)
        kf = kh * (1.0 + (ah - 1.0) * kaw_ref[:, sl])
        bh = kk * ah
        cum_end = cumh[C - 1:C, :]
        e_neg = jnp.exp(-cumh)
        e_end = jnp.exp(cum_end - cumh)
        at = (-kk) * jnp.exp(cumh - ldh)
        rt = rh * jnp.exp(cumh)
        ar = jnp.concatenate([at, rt], axis=0).astype(BF16)
        bt = (bh * e_neg).astype(BF16)
        kt = (kf * e_neg).astype(BF16)
        vb = vh.astype(BF16)
        S = st_ref[0, h]
        mb = _nt(ar, bt)
        mk = _nt(ar, kt)
        w1 = _nt(ar, S.astype(BF16))
        lab = jnp.where(strict, mb[:C], 0.0)
        lak = jnp.where(strict, mk[:C], 0.0)
        mrb = jnp.where(incl, mb[C:], 0.0)
        mrk = jnp.where(incl, mk[C:], 0.0)
        x = lab
        tinv = eye + lab
        for _ in range(n_sq):
            xb = x.astype(BF16)
            x = _nn(xb, xb)
            tinv = tinv + _nn(tinv.astype(BF16), x.astype(BF16))
        rhs = w1[:C] + _nn(lak.astype(BF16), vb)
        u = _nn(tinv.astype(BF16), rhs.astype(BF16))
        ub = u.astype(BF16)
        y = w1[C:] + _nn(mrb.astype(BF16), ub) + _nn(mrk.astype(BF16), vb)
        s_new = (S * jnp.exp(cum_end)
                 + _tn(ub, (bh * e_end).astype(BF16))
                 + _tn(vb, (kf * e_end).astype(BF16)))
        st_ref[0, h] = s_new
        mean = jnp.mean(y, axis=-1, keepdims=True)
        d = y - mean
        var = jnp.mean(d * d, axis=-1, keepdims=True)
        yn = (d * lax.rsqrt(var + GN_EPS)) * gg_ref[:, sl] + gb_ref[:, sl]
        bonus = jnp.sum(rh * kf * rkw_ref[:, sl], axis=-1, keepdims=True) * vh
        o_ref[:, sl] = ((yn + bonus) * g[:, sl]).astype(o_ref.dtype)


def _rwkv(proj, prev, state0, B, S, C, HB, pp, out_dtype):
    HW = HB * HEAD
    nc = S // C
    nhg = RWKV_HEADS // HB
    kb = RWKV_WIDTH // HW
    lcol = COL_LORA // LORA_PAD

    def row(b, hg, c):
        return b * nc + c

    in_specs = [
        pl.BlockSpec((C, HW), lambda b, hg, c: (row(b, hg, c), hg)),
        pl.BlockSpec((C, HW), lambda b, hg, c: (row(b, hg, c), kb + hg)),
        pl.BlockSpec((C, HW), lambda b, hg, c: (row(b, hg, c), 2 * kb + hg)),
        pl.BlockSpec((C, LORA_PAD), lambda b, hg, c: (row(b, hg, c), lcol)),
        pl.BlockSpec((1, 1, HW), lambda b, hg, c: (b, 0, hg)),
        pl.BlockSpec((1, 1, HW), lambda b, hg, c: (b, 0, kb + hg)),
        pl.BlockSpec((1, 1, HW), lambda b, hg, c: (b, 0, 2 * kb + hg)),
        pl.BlockSpec((1, 1, LORA_PAD), lambda b, hg, c: (b, 0, lcol)),
        pl.BlockSpec((1, HW), lambda b, hg, c: (0, hg)),
        pl.BlockSpec((1, HW), lambda b, hg, c: (0, kb + hg)),
        pl.BlockSpec((1, HW), lambda b, hg, c: (0, 2 * kb + hg)),
        pl.BlockSpec((1, LORA_PAD), lambda b, hg, c: (0, lcol)),
    ]
    args = [proj, proj, proj, proj, prev, prev, prev, prev, pp['mu'], pp['mu'], pp['mu'], pp['mu']]
    for name in ('w0', 'a0', 'k_k', 'k_a', 'r_k', 'gn_g', 'gn_b'):
        in_specs.append(pl.BlockSpec((1, HW), lambda b, hg, c: (0, hg)))
        args.append(pp[name])
    for name in ('w_w2', 'w_a2', 'w_g2'):
        in_specs.append(pl.BlockSpec((LORA_PAD, HW), lambda b, hg, c: (0, hg)))
        args.append(pp[name])
    if state0 is not None:
        in_specs.append(pl.BlockSpec((1, HB, HEAD, HEAD), lambda b, hg, c: (b, hg, 0, 0)))
        args.append(state0)
    return pl.pallas_call(
        functools.partial(_rwkv_kernel, C=C, HB=HB, has_state=state0 is not None),
        grid=(B, nhg, nc),
        in_specs=in_specs,
        out_specs=[
            pl.BlockSpec((C, HW), lambda b, hg, c: (row(b, hg, c), hg)),
            pl.BlockSpec((1, HB, HEAD, HEAD), lambda b, hg, c: (b, hg, 0, 0)),
        ],
        out_shape=[
            jax.ShapeDtypeStruct((B * S, RWKV_WIDTH), out_dtype),
            jax.ShapeDtypeStruct((B, RWKV_HEADS, HEAD, HEAD), F32),
        ],
        scratch_shapes=[pltpu.VMEM((1, HW), F32), pltpu.VMEM((1, HW), F32), pltpu.VMEM((1, HW), F32),
                        pltpu.VMEM((1, LORA_PAD), F32)],
        compiler_params=_cp(("parallel", "parallel", "arbitrary")),
        name="rwkv7_mix",
    )(*args)


def _swap_halves(x):
    half = x.shape[-1] // 2
    return jnp.concatenate([x[:, half:], x[:, :half]], axis=1)


def _kvprep_kernel(kv_ref, g_ref, cos_ref, sin_ref, c_ref, kp_ref, cb_ref, kpb_ref):
    kv = kv_ref[...]
    lat = kv[:, :KV_LORA]
    ms = jnp.mean(lat * lat, axis=-1, keepdims=True)
    c = (lat * lax.rsqrt(ms + NORM_EPS)) * g_ref[...]
    xr = kv[:, KV_LORA:KV_LORA + ROPE]
    kp = xr * cos_ref[...] + _swap_halves(xr) * sin_ref[...]
    c_ref[...] = c
    kp_ref[...] = kp
    cb_ref[...] = c.astype(BF16)
    kpb_ref[...] = kp.astype(BF16)


def _kvprep(proj, g, cos64, sin64, tm):
    N = proj.shape[0]
    nper = cos64.shape[0] // tm
    return pl.pallas_call(
        _kvprep_kernel,
        grid=(N // tm,),
        in_specs=[
            pl.BlockSpec((tm, KV_PAD), lambda i: (i, COL_KV // KV_PAD)),
            pl.BlockSpec((1, KV_LORA), lambda i: (0, 0)),
            pl.BlockSpec((tm, ROPE), lambda i: (i % nper, 0)),
            pl.BlockSpec((tm, ROPE), lambda i: (i % nper, 0)),
        ],
        out_specs=[
            pl.BlockSpec((tm, KV_LORA), lambda i: (i, 0)),
            pl.BlockSpec((tm, ROPE), lambda i: (i, 0)),
            pl.BlockSpec((tm, KV_LORA), lambda i: (i, 0)),
            pl.BlockSpec((tm, ROPE), lambda i: (i, 0)),
        ],
        out_shape=[
            jax.ShapeDtypeStruct((N, KV_LORA), F32),
            jax.ShapeDtypeStruct((N, ROPE), F32),
            jax.ShapeDtypeStruct((N, KV_LORA), BF16),
            jax.ShapeDtypeStruct((N, ROPE), BF16),
        ],
        compiler_params=_cp(("parallel",)),
        name="mla_kv_prep",
    )(proj, g.reshape(1, -1), cos64, sin64)


def _prep_queries(qn_ref, qr_ref, cos_ref, sin_ref, wuk_ref, qlat_s, qpe_s, tq):
    qn = qn_ref[...]
    for h in range(MLA_HEADS):
        ql = _nn(qn[:, h * NOPE:(h + 1) * NOPE].astype(BF16), wuk_ref[h])
        qlat_s[h * tq:(h + 1) * tq, :] = (ql * MLA_SCALE).astype(BF16)
    qr = qr_ref[...]
    width = qr.shape[1]
    lane = lax.broadcasted_iota(jnp.int32, qr.shape, 1)
    half = ROPE // 2
    rot = jnp.where(lane % ROPE < half, pltpu.roll(qr, width - half, axis=1), pltpu.roll(qr, half, axis=1))
    qp = (qr * cos_ref[...] + rot * sin_ref[...]) * MLA_SCALE
    for h in range(MLA_HEADS):
        qpe_s[h * tq:(h + 1) * tq, :] = qp[:, h * ROPE:(h + 1) * ROPE].astype(BF16)


def _online_softmax_step(s, kcb, m_s, l_s, acc_s):
    m_old = m_s[...]
    m_new = jnp.maximum(m_old, jnp.max(s, axis=-1, keepdims=True))
    alpha = jnp.exp(m_old - m_new)
    p = jnp.exp(s - m_new)
    l_s[...] = alpha * l_s[...] + jnp.sum(p, axis=-1, keepdims=True)
    acc_s[...] = alpha * acc_s[...] + _nn(p.astype(BF16), kcb)
    m_s[...] = m_new


def _finish_attention(o_ref, wuv_ref, l_s, acc_s, tq):
    for h in range(MLA_HEADS):
        rows = slice(h * tq, (h + 1) * tq)
        o_lat = acc_s[rows, :] / l_s[rows, :]
        o_ref[:, h * V_DIM:(h + 1) * V_DIM] = _nn(o_lat.astype(BF16), wuv_ref[h]).astype(o_ref.dtype)


def _attn_prompt_kernel(qn_ref, qr_ref, cos_ref, sin_ref, kc_ref, kp_ref, wuk_ref, wuv_ref, o_ref,
                        qlat_s, qpe_s, m_s, l_s, acc_s, *, tq, tk):
    i = pl.program_id(1)
    _prep_queries(qn_ref, qr_ref, cos_ref, sin_ref, wuk_ref, qlat_s, qpe_s, tq)
    m_s[...] = jnp.full(m_s.shape, -jnp.inf, F32)
    l_s[...] = jnp.zeros(l_s.shape, F32)
    acc_s[...] = jnp.zeros(acc_s.shape, F32)
    rows = MLA_HEADS * tq
    qpos = i * tq + lax.broadcasted_iota(jnp.int32, (rows, tk), 0) % tq
    kidx = lax.broadcasted_iota(jnp.int32, (rows, tk), 1)

    def body(j, carry):
        start = pl.multiple_of(j * tk, tk)
        kcb = kc_ref[pl.ds(start, tk), :]
        s = _nt(qlat_s[...], kcb) + _nt(qpe_s[...], kp_ref[pl.ds(start, tk), :])
        s = jnp.where(kidx + j * tk <= qpos, s, -jnp.inf)
        _online_softmax_step(s, kcb, m_s, l_s, acc_s)
        return carry

    lax.fori_loop(0, (i * tq + tq + tk - 1) // tk, body, 0)
    _finish_attention(o_ref, wuv_ref, l_s, acc_s, tq)


def _attn_prompt(q, kcb, kpb, cosq, sinq, wuk, wuv, B, S, tq, tk):
    nq = S // tq
    rows = MLA_HEADS * tq
    return pl.pallas_call(
        functools.partial(_attn_prompt_kernel, tq=tq, tk=tk),
        grid=(B, nq),
        in_specs=[
            pl.BlockSpec((tq, MLA_HEADS * NOPE), lambda b, i: (b * nq + i, 0)),
            pl.BlockSpec((tq, MLA_HEADS * ROPE), lambda b, i: (b * nq + i, (MLA_HEADS * NOPE) // (MLA_HEADS * ROPE))),
            pl.BlockSpec((tq, MLA_HEADS * ROPE), lambda b, i: (i, 0)),
            pl.BlockSpec((tq, MLA_HEADS * ROPE), lambda b, i: (i, 0)),
            pl.BlockSpec((S, KV_LORA), lambda b, i: (b, 0)),
            pl.BlockSpec((S, ROPE), lambda b, i: (b, 0)),
            pl.BlockSpec((MLA_HEADS, NOPE, KV_LORA), lambda b, i: (0, 0, 0)),
            pl.BlockSpec((MLA_HEADS, KV_LORA, V_DIM), lambda b, i: (0, 0, 0)),
        ],
        out_specs=pl.BlockSpec((tq, MLA_HEADS * V_DIM), lambda b, i: (b * nq + i, 0)),
        out_shape=jax.ShapeDtypeStruct((B * S, MLA_HEADS * V_DIM), BF16),
        scratch_shapes=[
            pltpu.VMEM((rows, KV_LORA), BF16), pltpu.VMEM((rows, ROPE), BF16),
            pltpu.VMEM((rows, 1), F32), pltpu.VMEM((rows, 1), F32), pltpu.VMEM((rows, KV_LORA), F32),
        ],
        compiler_params=_cp(("parallel", "arbitrary")),
        name="mla_attn_prompt",
    )(q, q, cosq, sinq, kcb, kpb, wuk, wuv)


def _attn_sample_kernel(pt_ref, qn_ref, qr_ref, cos_ref, sin_ref, cn_ref, kn_ref, wuk_ref, wuv_ref,
                        ckv_hbm, ckr_hbm, o_ref,
                        qlat_s, qpe_s, m_s, l_s, acc_s, cbuf, rbuf, nc_s, nk_s, sem_c, sem_r,
                        *, T, G, n_pages):
    b = pl.program_id(0)
    nb = pl.num_programs(0)
    ngrp = n_pages // G
    rows = MLA_HEADS * T

    def page_copies(seq, grp, slot, p):
        page = pt_ref[seq, grp * G + p]
        cc = pltpu.make_async_copy(ckv_hbm.at[page], cbuf.at[slot, pl.ds(p * PAGE, PAGE)], sem_c.at[slot])
        cr = pltpu.make_async_copy(ckr_hbm.at[page], rbuf.at[slot, pl.ds(p * PAGE, PAGE)], sem_r.at[slot])
        return cc, cr

    def start_group(seq, grp, slot):
        for p in range(G):
            cc, cr = page_copies(seq, grp, slot, p)
            cc.start()
            cr.start()

    def wait_group(seq, grp, slot):
        for p in range(G):
            cc, cr = page_copies(seq, grp, slot, p)
            cc.wait()
            cr.wait()

    @pl.when(b == 0)
    def _():
        start_group(0, 0, 0)

    _prep_queries(qn_ref, qr_ref, cos_ref, sin_ref, wuk_ref, qlat_s, qpe_s, T)
    m_s[...] = jnp.full(m_s.shape, -jnp.inf, F32)
    l_s[...] = jnp.zeros(l_s.shape, F32)
    acc_s[...] = jnp.zeros(acc_s.shape, F32)

    def body(grp, carry):
        gg = b * ngrp + grp
        slot = gg % 2
        wait_group(b, grp, slot)

        @pl.when(grp + 1 < ngrp)
        def _():
            start_group(b, grp + 1, 1 - slot)

        @pl.when(jnp.logical_and(grp + 1 == ngrp, b + 1 < nb))
        def _():
            start_group(b + 1, 0, 1 - slot)

        kcb = cbuf[slot].astype(BF16)
        s = _nt(qlat_s[...], kcb) + _nt(qpe_s[...], rbuf[slot].astype(BF16))
        _online_softmax_step(s, kcb, m_s, l_s, acc_s)
        return carry

    lax.fori_loop(0, ngrp, body, 0)

    nc_s[...] = jnp.zeros(nc_s.shape, F32)
    nk_s[...] = jnp.zeros(nk_s.shape, F32)
    nc_s[0:T, :] = cn_ref[...]
    nk_s[0:T, :] = kn_ref[...]
    kcb = nc_s[...].astype(BF16)
    s = _nt(qlat_s[...], kcb) + _nt(qpe_s[...], nk_s[...].astype(BF16))
    qt = lax.broadcasted_iota(jnp.int32, (rows, PAGE), 0) % T
    kidx = lax.broadcasted_iota(jnp.int32, (rows, PAGE), 1)
    s = jnp.where(kidx <= qt, s, -jnp.inf)
    _online_softmax_step(s, kcb, m_s, l_s, acc_s)
    _finish_attention(o_ref, wuv_ref, l_s, acc_s, T)


def _attn_sample(q, c_new, k_new, cosq, sinq, wuk, wuv, page_table, cache_kv, cache_kr, DB, T, G):
    n_pages = page_table.shape[1]
    rows = MLA_HEADS * T
    grid_spec = pltpu.PrefetchScalarGridSpec(
        num_scalar_prefetch=1,
        grid=(DB,),
        in_specs=[
            pl.BlockSpec((T, MLA_HEADS * NOPE), lambda b, pt: (b, 0)),
            pl.BlockSpec((T, MLA_HEADS * ROPE), lambda b, pt: (b, (MLA_HEADS * NOPE) // (MLA_HEADS * ROPE))),
            pl.BlockSpec((T, MLA_HEADS * ROPE), lambda b, pt: (0, 0)),
            pl.BlockSpec((T, MLA_HEADS * ROPE), lambda b, pt: (0, 0)),
            pl.BlockSpec((T, KV_LORA), lambda b, pt: (b, 0)),
            pl.BlockSpec((T, ROPE), lambda b, pt: (b, 0)),
            pl.BlockSpec((MLA_HEADS, NOPE, KV_LORA), lambda b, pt: (0, 0, 0)),
            pl.BlockSpec((MLA_HEADS, KV_LORA, V_DIM), lambda b, pt: (0, 0, 0)),
            pl.BlockSpec(memory_space=pl.ANY),
            pl.BlockSpec(memory_space=pl.ANY),
        ],
        out_specs=pl.BlockSpec((T, MLA_HEADS * V_DIM), lambda b, pt: (b, 0)),
        scratch_shapes=[
            pltpu.VMEM((rows, KV_LORA), BF16), pltpu.VMEM((rows, ROPE), BF16),
            pltpu.VMEM((rows, 1), F32), pltpu.VMEM((rows, 1), F32), pltpu.VMEM((rows, KV_LORA), F32),
            pltpu.VMEM((2, G * PAGE, KV_LORA), F32), pltpu.VMEM((2, G * PAGE, ROPE), F32),
            pltpu.VMEM((PAGE, KV_LORA), F32), pltpu.VMEM((PAGE, ROPE), F32),
            pltpu.SemaphoreType.DMA((2,)), pltpu.SemaphoreType.DMA((2,)),
        ],
    )
    return pl.pallas_call(
        functools.partial(_attn_sample_kernel, T=T, G=G, n_pages=n_pages),
        grid_spec=grid_spec,
        out_shape=jax.ShapeDtypeStruct((DB * T, MLA_HEADS * V_DIM), F32),
        compiler_params=_cp(("arbitrary",)),
        name="mla_attn_sample",
    )(page_table, q, q, cosq, sinq, c_new, k_new, wuk, wuv, cache_kv, cache_kr)


def _memblock_kernel(x_ref, mk_ref, mv_ref, gm_ref, wq_ref, wo_ref, ge_ref, wr_ref, br_ref,
                     x2_ref, hn_ref, eid_ref, wts_ref, o_s, *, nb, tq):
    x = x_ref[...]
    ms = jnp.mean(x * x, axis=-1, keepdims=True)
    xn = ((x * lax.rsqrt(ms + NORM_EPS)) * gm_ref[...]).astype(BF16)
    q = _nn(xn, wq_ref[...]) * MEM_SCALE
    for n in range(nb):
        rows = slice(n * tq, (n + 1) * tq)
        for h in range(MEM_HEADS):
            cols = slice(h * MEM_HEAD_DIM, (h + 1) * MEM_HEAD_DIM)
            s = _nt(q[rows, cols].astype(BF16), mk_ref[n, :, cols].astype(BF16))
            m = jnp.max(s, axis=-1, keepdims=True)
            p = jnp.exp(s - m)
            p = p / jnp.sum(p, axis=-1, keepdims=True)
            o_s[rows, cols] = _nn(p.astype(BF16), mv_ref[n, :, cols].astype(BF16))
    x2 = x + _nn(o_s[...].astype(BF16), wo_ref[...])
    x2_ref[...] = x2
    ms2 = jnp.mean(x2 * x2, axis=-1, keepdims=True)
    hn = (x2 * lax.rsqrt(ms2 + NORM_EPS)) * ge_ref[...]
    hn_ref[...] = hn

    logits = jnp.dot(hn, wr_ref[...], precision=HIGHEST, preferred_element_type=F32) + br_ref[...]
    lane = lax.broadcasted_iota(jnp.int32, logits.shape, 1)
    big = jnp.int32(ROUTER_PAD)
    lc = jnp.where(lane < N_GROUPS, logits, -jnp.inf)
    mc = jnp.max(lc, axis=-1, keepdims=True)
    grp = jnp.min(jnp.where(lc == mc, lane, big), axis=-1, keepdims=True)
    gate_c = 1.0 / jnp.sum(jnp.exp(lc - mc), axis=-1, keepdims=True)
    lo = N_GROUPS + EPG * grp
    fmask = jnp.logical_and(lane >= lo, lane < lo + EPG)
    lf = jnp.where(fmask, logits, -jnp.inf)
    mf = jnp.max(lf, axis=-1, keepdims=True)
    ef = jnp.exp(lf - mf)
    pf = ef / jnp.sum(ef, axis=-1, keepdims=True)
    pf = jnp.where(fmask, pf, -1.0)
    v1 = jnp.max(pf, axis=-1, keepdims=True)
    i1 = jnp.min(jnp.where(pf == v1, lane, big), axis=-1, keepdims=True)
    pf2 = jnp.where(lane == i1, -1.0, pf)
    v2 = jnp.max(pf2, axis=-1, keepdims=True)
    i2 = jnp.min(jnp.where(pf2 == v2, lane, big), axis=-1, keepdims=True)
    den = v1 + v2
    eid_ref[...] = jnp.where(lane == 0, i1 - N_GROUPS, jnp.where(lane == 1, i2 - N_GROUPS, 0))
    wts_ref[...] = jnp.where(lane == 0, gate_c * v1 / den, jnp.where(lane == 1, gate_c * v2 / den, 0.0))


def _memblock(x, mem_k, mem_v, lw, nb, tq, tiles_per_mem):
    N = x.shape[0]
    R = nb * tq
    return pl.pallas_call(
        functools.partial(_memblock_kernel, nb=nb, tq=tq),
        grid=(N // R,),
        in_specs=[
            pl.BlockSpec((R, D_MODEL), lambda i: (i, 0)),
            pl.BlockSpec((nb, MEM_TOKENS, MEM_WIDTH), lambda i: (i // tiles_per_mem, 0, 0)),
            pl.BlockSpec((nb, MEM_TOKENS, MEM_WIDTH), lambda i: (i // tiles_per_mem, 0, 0)),
            pl.BlockSpec((1, D_MODEL), lambda i: (0, 0)),
            pl.BlockSpec((D_MODEL, MEM_WIDTH), lambda i: (0, 0)),
            pl.BlockSpec((MEM_WIDTH, D_MODEL), lambda i: (0, 0)),
            pl.BlockSpec((1, D_MODEL), lambda i: (0, 0)),
            pl.BlockSpec((D_MODEL, ROUTER_PAD), lambda i: (0, 0)),
            pl.BlockSpec((1, ROUTER_PAD), lambda i: (0, 0)),
        ],
        out_specs=[
            pl.BlockSpec((R, D_MODEL), lambda i: (i, 0)),
            pl.BlockSpec((R, D_MODEL), lambda i: (i, 0)),
            pl.BlockSpec((R, ROUTER_PAD), lambda i: (i, 0)),
            pl.BlockSpec((R, ROUTER_PAD), lambda i: (i, 0)),
        ],
        out_shape=[
            jax.ShapeDtypeStruct((N, D_MODEL), F32),
            jax.ShapeDtypeStruct((N, D_MODEL), F32),
            jax.ShapeDtypeStruct((N, ROUTER_PAD), jnp.int32),
            jax.ShapeDtypeStruct((N, ROUTER_PAD), F32),
        ],
        scratch_shapes=[pltpu.VMEM((R, MEM_WIDTH), F32)],
        compiler_params=_cp(("parallel",)),
        name="mem_attn_router",
    )(x, mem_k, mem_v, lw['ln_mem_g'], lw['w_mq'], lw['w_mo'], lw['ln_moe_g'], lw['w_router'], lw['b_router'])


def _moe_kernel(be_ref, nu_ref, tok_ref, hn_hbm, wg_ref, wu_ref, wd_ref, y_ref,
                xbuf, wgb, wub, wdb, sem):
    i = pl.program_id(0)
    nused = nu_ref[0]

    def row_copy(which, slot, r):
        return pltpu.make_async_copy(hn_hbm.at[pl.ds(tok_ref[0, which, r], 1)], xbuf.at[slot, pl.ds(r, 1)],
                                     sem.at[slot])

    def start_rows(which, slot):
        def body(r, carry):
            row_copy(which, slot, r).start()
            return carry
        lax.fori_loop(0, MOE_BLOCK, body, 0)

    def wait_rows(which, slot):
        def body(r, carry):
            row_copy(which, slot, r).wait()
            return carry
        lax.fori_loop(0, MOE_BLOCK, body, 0)

    slot = i % 2

    @pl.when(jnp.logical_and(i == 0, nused > 0))
    def _():
        start_rows(0, 0)

    @pl.when(i < nused)
    def _():
        wait_rows(0, slot)

        @pl.when(i + 1 < nused)
        def _():
            start_rows(1, 1 - slot)

        changed = jnp.logical_or(i == 0, be_ref[i] != be_ref[jnp.maximum(i - 1, 0)])

        @pl.when(changed)
        def _():
            wgb[...] = wg_ref[0].astype(BF16)
            wub[...] = wu_ref[0].astype(BF16)
            wdb[...] = wd_ref[0].astype(BF16)

        xb = xbuf[slot].astype(BF16)
        gate = _nn(xb, wgb[...])
        up = _nn(xb, wub[...])
        hid = (gate * jax.nn.sigmoid(gate)) * up
        y_ref[...] = _nn(hid.astype(BF16), wdb[...])

    @pl.when(i >= nused)
    def _():
        y_ref[...] = jnp.zeros(y_ref.shape, F32)


def _moe_experts(hn, slot_tok, block_e, nused, w_gate, w_up, w_down):
    n_blocks = block_e.shape[0]
    tok2 = jnp.stack([slot_tok, jnp.roll(slot_tok, -1, axis=0)], axis=1)
    grid_spec = pltpu.PrefetchScalarGridSpec(
        num_scalar_prefetch=2,
        grid=(n_blocks,),
        in_specs=[
            pl.BlockSpec((1, 2, MOE_BLOCK), lambda i, be, nu: (i, 0, 0), memory_space=pltpu.SMEM),
            pl.BlockSpec(memory_space=pl.ANY),
            pl.BlockSpec((1, D_MODEL, EXPERT_FF), lambda i, be, nu: (be[i], 0, 0)),
            pl.BlockSpec((1, D_MODEL, EXPERT_FF), lambda i, be, nu: (be[i], 0, 0)),
            pl.BlockSpec((1, EXPERT_FF, D_MODEL), lambda i, be, nu: (be[i], 0, 0)),
        ],
        out_specs=pl.BlockSpec((MOE_BLOCK, D_MODEL), lambda i, be, nu: (i, 0)),
        scratch_shapes=[
            pltpu.VMEM((2, MOE_BLOCK, D_MODEL), F32),
            pltpu.VMEM((D_MODEL, EXPERT_FF), BF16), pltpu.VMEM((D_MODEL, EXPERT_FF), BF16),
            pltpu.VMEM((EXPERT_FF, D_MODEL), BF16),
            pltpu.SemaphoreType.DMA((2,)),
        ],
    )
    return pl.pallas_call(
        _moe_kernel,
        grid_spec=grid_spec,
        out_shape=jax.ShapeDtypeStruct((n_blocks * MOE_BLOCK, D_MODEL), F32),
        compiler_params=_cp(("arbitrary",)),
        name="moe_experts",
    )(block_e, nused, tok2, hn, w_gate, w_up, w_down)


def _combine_kernel(pos_ref, x_ref, w_ref, g_ref, yb_hbm, o_ref, rbuf, sem, *, R):
    i = pl.program_id(0)
    n = pl.num_programs(0)

    def row_copy(which, slot, r):
        return pltpu.make_async_copy(yb_hbm.at[pl.ds(pos_ref[0, which, r], 1)],
                                     rbuf.at[slot, pl.ds(r, 1)], sem.at[slot])

    def start_rows(which, slot):
        def body(r, carry):
            row_copy(which, slot, r).start()
            return carry
        lax.fori_loop(0, 2 * R, body, 0)

    def wait_rows(which, slot):
        def body(r, carry):
            row_copy(which, slot, r).wait()
            return carry
        lax.fori_loop(0, 2 * R, body, 0)

    slot = i % 2

    @pl.when(i == 0)
    def _():
        start_rows(0, 0)

    wait_rows(0, slot)

    @pl.when(i + 1 < n)
    def _():
        start_rows(1, 1 - slot)

    w = w_ref[...]
    x = x_ref[...] + (w[:, 0:1] * rbuf[slot, 0:R, :] + w[:, 1:2] * rbuf[slot, R:2 * R, :])
    ms = jnp.mean(x * x, axis=-1, keepdims=True)
    o_ref[...] = (x * lax.rsqrt(ms + NORM_EPS)) * g_ref[...]


def _moe_combine(x2, wts, pos, yb, final_g, R):
    N = x2.shape[0]
    nt = N // R
    p = pos.reshape(nt, R, 2).transpose(0, 2, 1).reshape(nt, 2 * R)
    p2 = jnp.stack([p, jnp.roll(p, -1, axis=0)], axis=1)
    return pl.pallas_call(
        functools.partial(_combine_kernel, R=R),
        grid=(nt,),
        in_specs=[
            pl.BlockSpec((1, 2, 2 * R), lambda i: (i, 0, 0), memory_space=pltpu.SMEM),
            pl.BlockSpec((R, D_MODEL), lambda i: (i, 0)),
            pl.BlockSpec((R, ROUTER_PAD), lambda i: (i, 0)),
            pl.BlockSpec((1, D_MODEL), lambda i: (0, 0)),
            pl.BlockSpec(memory_space=pl.ANY),
        ],
        out_specs=pl.BlockSpec((R, D_MODEL), lambda i: (i, 0)),
        out_shape=jax.ShapeDtypeStruct((N, D_MODEL), F32),
        scratch_shapes=[pltpu.VMEM((2, 2 * R, D_MODEL), F32), pltpu.SemaphoreType.DMA((2,))],
        compiler_params=_cp(("arbitrary",)),
        name="moe_combine_norm",
    )(p2, x2, wts, final_g.reshape(1, -1), yb)


def _moe_dispatch_tables(eid, n_tokens):
    A = n_tokens * 2
    flat_e = eid.reshape(-1)
    flat_tok = jnp.arange(A, dtype=jnp.int32) // 2
    onehot = (flat_e[:, None] == jnp.arange(N_EXPERTS, dtype=jnp.int32)[None, :]).astype(jnp.int32)
    ranks = jnp.cumsum(onehot, axis=0)
    counts = ranks[-1]
    rank = jnp.take_along_axis(ranks, flat_e[:, None], axis=1)[:, 0] - 1
    padded = (counts + MOE_BLOCK - 1) // MOE_BLOCK * MOE_BLOCK
    pends = jnp.cumsum(padded)
    pstarts = pends - padded
    dest = pstarts[flat_e] + rank
    n_blocks = -(-A // MOE_BLOCK) + N_EXPERTS
    n_slots = n_blocks * MOE_BLOCK
    slot_tok = jnp.zeros((n_slots,), jnp.int32).at[dest].set(flat_tok)
    block_start = jnp.arange(n_blocks, dtype=jnp.int32) * MOE_BLOCK
    block_e = jnp.minimum(jnp.searchsorted(pends, block_start, side='right'), N_EXPERTS - 1).astype(jnp.int32)
    nused = (pends[-1:] // MOE_BLOCK).astype(jnp.int32)
    return slot_tok.reshape(n_blocks, MOE_BLOCK), block_e, nused, dest.reshape(n_tokens, 2)


def _rope_tables(pos, reps):
    half = ROPE // 2
    inv = ROPE_THETA ** (-jnp.arange(half, dtype=F32) / half)
    ang = pos.astype(F32)[:, None] * inv[None, :]
    cos, sin = jnp.cos(ang), jnp.sin(ang)
    cos64 = jnp.concatenate([cos, cos], axis=1)
    sin64 = jnp.concatenate([-sin, sin], axis=1)
    return cos64, sin64, jnp.tile(cos64, (1, reps)), jnp.tile(sin64, (1, reps))


def _to_layout(v):
    out = jnp.zeros(v.shape[:-1] + (PROJ_W,), F32)
    out = out.at[..., COL_MAIN:COL_MAIN + 3 * RWKV_WIDTH].set(v[..., :3 * RWKV_WIDTH])
    return out.at[..., COL_LORA:COL_LORA + LORA_W].set(v[..., 3 * RWKV_WIDTH:])


def _from_layout(p):
    return jnp.concatenate([p[..., :3 * RWKV_WIDTH], p[..., COL_LORA:COL_LORA + LORA_W]], axis=-1)


def _pad_rows(w, start, total):
    return jnp.zeros((total, w.shape[1]), w.dtype).at[start:start + w.shape[0]].set(w)


def _prepare(ln_mix_g, w_in, rwkv_mu, rwkv_w0, rwkv_w_w2, rwkv_a0, rwkv_w_a2, rwkv_w_g2, rwkv_k_k, rwkv_k_a,
             rwkv_r_k, rwkv_gn_g, rwkv_gn_b, mla_q_norm_g, mla_w_qb, mla_kv_norm_g, mla_w_kvb, w_out, ln_mem_g,
             mem_norm_g, w_mq, w_mk, w_mv, w_mo, ln_moe_g, w_router_c, b_router_c, w_router_f, b_router_f):
    i3 = 3 * RWKV_WIDTH
    w_in_wide = jnp.zeros((D_MODEL, PROJ_W), F32)
    w_in_wide = w_in_wide.at[:, COL_MAIN:COL_MAIN + i3].set(w_in[:, :i3])
    w_in_wide = w_in_wide.at[:, COL_LORA:COL_LORA + LORA_W].set(w_in[:, i3:RWKV_PROJ])
    w_in_wide = w_in_wide.at[:, COL_QA:COL_QA + Q_LORA].set(w_in[:, RWKV_PROJ:RWKV_PROJ + Q_LORA])
    w_in_wide = w_in_wide.at[:, COL_KV:COL_KV + KV_LORA + ROPE].set(w_in[:, RWKV_PROJ + Q_LORA:])
    row = lambda v: v.reshape(1, -1).astype(F32)
    rw = {
        'mu': _to_layout(rwkv_mu.reshape(1, -1)),
        'w0': row(rwkv_w0), 'a0': row(rwkv_a0), 'k_k': row(rwkv_k_k), 'k_a': row(rwkv_k_a),
        'r_k': row(rwkv_r_k), 'gn_g': row(rwkv_gn_g), 'gn_b': row(rwkv_gn_b),
        'w_w2': _pad_rows(rwkv_w_w2, 0, LORA_PAD).astype(BF16),
        'w_a2': _pad_rows(rwkv_w_a2, W_LORA, LORA_PAD).astype(BF16),
        'w_g2': _pad_rows(rwkv_w_g2, W_LORA + A_LORA, LORA_PAD).astype(BF16),
    }
    qb = mla_w_qb.reshape(Q_LORA, MLA_HEADS, NOPE + ROPE)
    w_qb = jnp.concatenate([qb[:, :, :NOPE].reshape(Q_LORA, -1), qb[:, :, NOPE:].reshape(Q_LORA, -1)], axis=1)
    kvb = mla_w_kvb.reshape(KV_LORA, MLA_HEADS, NOPE + V_DIM)
    w_router = jnp.zeros((D_MODEL, ROUTER_PAD), F32)
    w_router = w_router.at[:, :N_GROUPS].set(w_router_c).at[:, N_GROUPS:N_GROUPS + N_EXPERTS].set(w_router_f)
    b_router = jnp.zeros((1, ROUTER_PAD), F32)
    b_router = b_router.at[0, :N_GROUPS].set(b_router_c).at[0, N_GROUPS:N_GROUPS + N_EXPERTS].set(b_router_f)
    return {
        'ln_mix_g': ln_mix_g, 'w_in': w_in_wide.astype(BF16), 'rwkv': rw,
        'q_norm_g': mla_q_norm_g, 'w_qb': w_qb.astype(BF16), 'kv_norm_g': mla_kv_norm_g,
        'w_uk': jnp.transpose(kvb[:, :, :NOPE], (1, 2, 0)).astype(BF16),
        'w_uv': jnp.transpose(kvb[:, :, NOPE:], (1, 0, 2)).astype(BF16),
        'w_out_r': w_out[:RWKV_WIDTH].astype(BF16), 'w_out_m': w_out[RWKV_WIDTH:].astype(BF16),
        'mem_norm_g': mem_norm_g, 'w_mkv': jnp.concatenate([w_mk, w_mv], axis=1).astype(BF16),
        'ln_mem_g': row(ln_mem_g), 'w_mq': w_mq.astype(BF16), 'w_mo': w_mo.astype(BF16),
        'ln_moe_g': row(ln_moe_g), 'w_router': w_router, 'b_router': b_router,
    }


def _decoder_group(x, B, S, prev, state0, mem_k, mem_v, lw, w_gate, w_up, w_down, final_g, cfg, attend):
    N = B * S
    proj = _mm([x], [lw['w_in']], g=lw['ln_mix_g'], tm=cfg['tm'], tn=512, name="in_proj")
    o_rwkv, st = _rwkv(proj, prev, state0, B, S, cfg['C'], cfg['HB'], lw['rwkv'], cfg['o_dtype'])
    c_kv, k_pe, ckb, kpb = _kvprep(proj, lw['kv_norm_g'], cfg['cos64'], cfg['sin64'], cfg['tkv'])
    q = _mm([proj], [lw['w_qb']], g=lw['q_norm_g'], tm=cfg['tm'], tn=512, x_cols=[COL_QA // Q_LORA], name="q_proj")
    o_mla = attend(q, c_kv, k_pe, ckb, kpb)
    x1 = _mm([o_rwkv, o_mla], [lw['w_out_r'], lw['w_out_m']], res=x, tm=cfg['tm'], tn=512, name="out_proj")
    x2, hn, eid, wts = _memblock(x1, mem_k, mem_v, lw, cfg['nb'], cfg['tq_mem'], cfg['tiles_per_mem'])
    slot_tok, block_e, nused, pos = _moe_dispatch_tables(eid[:, :2], N)
    yb = _moe_experts(hn, slot_tok, block_e, nused, w_gate, w_up, w_down)
    y = _moe_combine(x2, wts, pos, yb, final_g, min(MOE_BLOCK, N))
    shift = _from_layout(proj.reshape(B, S, PROJ_W)[:, -1])
    return y, c_kv, k_pe, st, shift


def kernel(x_prompt, x_sample, mem_prompt, cache_kv_latent, cache_k_rope, cache_mem_k, cache_mem_v, state_rwkv, state_rwkv_shift, page_table, ln_mix_g, w_in, rwkv_mu, rwkv_w0, rwkv_w_w2, rwkv_a0, rwkv_w_a2, rwkv_w_g2, rwkv_k_k, rwkv_k_a, rwkv_r_k, rwkv_gn_g, rwkv_gn_b, mla_q_norm_g, mla_w_qb, mla_kv_norm_g, mla_w_kvb, w_out, ln_mem_g, mem_norm_g, w_mq, w_mk, w_mv, w_mo, ln_moe_g, w_router_c, b_router_c, w_router_f, b_router_f, w_gate, w_up, w_down, final_norm_g):
    B, S, D = x_prompt.shape
    DB, T, _ = x_sample.shape
    assert w_in.shape[0] == 1, "single-layer step"
    past_len = page_table.shape[1] * PAGE
    lw = _prepare(ln_mix_g[0], w_in[0], rwkv_mu[0], rwkv_w0[0], rwkv_w_w2[0], rwkv_a0[0], rwkv_w_a2[0],
                  rwkv_w_g2[0], rwkv_k_k[0], rwkv_k_a[0], rwkv_r_k[0], rwkv_gn_g[0], rwkv_gn_b[0],
                  mla_q_norm_g[0], mla_w_qb[0], mla_kv_norm_g[0], mla_w_kvb[0], w_out[0], ln_mem_g[0],
                  mem_norm_g[0], w_mq[0], w_mk[0], w_mv[0], w_mo[0], ln_moe_g[0], w_router_c[0],
                  b_router_c[0], w_router_f[0], b_router_f[0])
    wg = w_gate.reshape(N_EXPERTS, D_MODEL, EXPERT_FF)
    wu = w_up.reshape(N_EXPERTS, D_MODEL, EXPERT_FF)
    wd = w_down.reshape(N_EXPERTS, EXPERT_FF, D_MODEL)
    n_phys = cache_kv_latent.shape[1]

    cos64_p, sin64_p, cosq_p, sinq_p = _rope_tables(jnp.arange(S), MLA_HEADS)
    mem_kv = _mm([mem_prompt.reshape(B * MEM_TOKENS, D)], [lw['w_mkv']], g=lw['mem_norm_g'],
                 tm=512, tn=512, name="mem_kv_proj")
    mem_k = mem_kv[:, :MEM_WIDTH].reshape(B, MEM_TOKENS, MEM_WIDTH)
    mem_v = mem_kv[:, MEM_WIDTH:].reshape(B, MEM_TOKENS, MEM_WIDTH)
    cfg_p = dict(tm=1024, C=64, HB=8, o_dtype=BF16, cos64=cos64_p, sin64=sin64_p, tkv=512,
                 nb=1, tq_mem=512, tiles_per_mem=S // 512)
    attend_p = lambda q, c_kv, k_pe, ckb, kpb: _attn_prompt(
        q, ckb, kpb, cosq_p, sinq_p, lw['w_uk'], lw['w_uv'], B, S, 128, 256)
    y_p, c_p, k_p, st_p, sh_p = _decoder_group(
        x_prompt.reshape(B * S, D), B, S, jnp.zeros((B, 1, PROJ_W), F32), None, mem_k, mem_v, lw,
        wg, wu, wd, final_norm_g, cfg_p, attend_p)

    cos64_s, sin64_s, cosq_s, sinq_s = _rope_tables(past_len + jnp.arange(T), MLA_HEADS)
    n_s = DB * T
    cfg_s = dict(tm=n_s, C=T, HB=16, o_dtype=F32, cos64=jnp.tile(cos64_s, (n_s // T, 1)),
                 sin64=jnp.tile(sin64_s, (n_s // T, 1)), tkv=n_s, nb=8, tq_mem=T, tiles_per_mem=1)
    attend_s = lambda q, c_kv, k_pe, ckb, kpb: _attn_sample(
        q, c_kv, k_pe, cosq_s, sinq_s, lw['w_uk'], lw['w_uv'], page_table,
        cache_kv_latent.reshape(n_phys, PAGE, KV_LORA), cache_k_rope.reshape(n_phys, PAGE, ROPE), DB, T, 8)
    y_s, c_s, k_s, st_s, sh_s = _decoder_group(
        x_sample.reshape(n_s, D), DB, T, _to_layout(state_rwkv_shift[0])[:, None, :],
        state_rwkv.reshape(DB, RWKV_HEADS, HEAD, HEAD),
        cache_mem_k.reshape(DB, MEM_TOKENS, MEM_WIDTH), cache_mem_v.reshape(DB, MEM_TOKENS, MEM_WIDTH),
        lw, wg, wu, wd, final_norm_g, cfg_s, attend_s)

    return (y_p.reshape(B, S, D), y_s.reshape(DB, T, D),
            c_p.reshape(1, B, S, KV_LORA), k_p.reshape(1, B, S, ROPE),
            mem_k.reshape(1, B, MEM_TOKENS, MEM_HEADS, MEM_HEAD_DIM),
            mem_v.reshape(1, B, MEM_TOKENS, MEM_HEADS, MEM_HEAD_DIM),
            st_p[None], sh_p[None],
            c_s.reshape(1, DB, T, KV_LORA), k_s.reshape(1, DB, T, ROPE), st_s[None], sh_s[None])
```

```python
import functools

import jax
import jax.numpy as jnp
from jax import lax
from jax.experimental import pallas as pl
from jax.experimental.pallas import tpu as pltpu

F32 = jnp.float32
BF16 = jnp.bfloat16
HIGHEST = lax.Precision.HIGHEST

D_MODEL = 2048
NORM_EPS = 1e-6
HEAD = 64
RWKV_WIDTH = 1024
RWKV_HEADS = 16
W_LORA, A_LORA, G_LORA = 96, 96, 256
LORA_W = W_LORA + A_LORA + G_LORA
LORA_PAD = 512
RWKV_PROJ = 3 * RWKV_WIDTH + LORA_W
GN_EPS = 64e-5
MLA_HEADS = 8
V_DIM = 128
NOPE = 128
ROPE = 64
Q_LORA = 512
KV_LORA = 256
KV_PAD = 512
MLA_SCALE = (NOPE + ROPE) ** -0.5
ROPE_THETA = 10000.0
PAGE = 128
MEM_TOKENS = 256
MEM_HEADS = 4
MEM_HEAD_DIM = 128
MEM_WIDTH = 512
MEM_SCALE = MEM_HEAD_DIM ** -0.5
N_GROUPS = 4
EPG = 8
N_EXPERTS = 32
EXPERT_FF = 512
MOE_BLOCK = 128
ROUTER_PAD = 128

COL_MAIN = 0
COL_QA = 3 * RWKV_WIDTH
COL_LORA = COL_QA + Q_LORA
COL_KV = COL_LORA + LORA_PAD
PROJ_W = COL_KV + KV_PAD

VMEM_LIMIT = 56 * 1024 * 1024


def _cp(sem, vmem=VMEM_LIMIT):
    return pltpu.CompilerParams(dimension_semantics=sem, vmem_limit_bytes=vmem)


def _nt(a, b):
    return lax.dot_general(a, b, (((1,), (1,)), ((), ())), preferred_element_type=F32)


def _tn(a, b):
    return lax.dot_general(a, b, (((0,), (0,)), ((), ())), preferred_element_type=F32)


def _nn(a, b):
    return jnp.dot(a, b, preferred_element_type=F32)


def _mm_kernel(*refs, n_pairs, has_norm, has_res):
    xs = refs[:n_pairs]
    ws = refs[n_pairs:2 * n_pairs]
    idx = 2 * n_pairs
    g_ref = res_ref = None
    if has_norm:
        g_ref = refs[idx]
        idx += 1
    if has_res:
        res_ref = refs[idx]
        idx += 1
    o_ref = refs[idx]
    if has_norm:
        xn_ref = refs[idx + 1]

        @pl.when(pl.program_id(1) == 0)
        def _():
            x = xs[0][...].astype(F32)
            ms = jnp.mean(x * x, axis=-1, keepdims=True)
            xn_ref[...] = ((x * lax.rsqrt(ms + NORM_EPS)) * g_ref[...]).astype(BF16)

        acc = _nn(xn_ref[...], ws[0][...])
    else:
        acc = _nn(xs[0][...].astype(BF16), ws[0][...])
        for x_ref, w_ref in zip(xs[1:], ws[1:]):
            acc = acc + _nn(x_ref[...].astype(BF16), w_ref[...])
    if has_res:
        acc = acc + res_ref[...]
    o_ref[...] = acc.astype(o_ref.dtype)


def _mm(xs, ws, *, g=None, res=None, out_dtype=F32, tm, tn, x_cols=None, name):
    n_pairs = len(xs)
    M = xs[0].shape[0]
    N = ws[0].shape[1]
    x_cols = x_cols or [0] * n_pairs
    assert M % tm == 0 and N % tn == 0
    in_specs, args = [], []
    for x, w, c in zip(xs, ws, x_cols):
        K = w.shape[0]
        in_specs.append(pl.BlockSpec((tm, K), functools.partial(lambda i, j, c: (i, c), c=c)))
        args.append(x)
    for w in ws:
        in_specs.append(pl.BlockSpec((w.shape[0], tn), lambda i, j: (0, j)))
        args.append(w)
    scratch = []
    if g is not None:
        assert n_pairs == 1
        in_specs.append(pl.BlockSpec((1, ws[0].shape[0]), lambda i, j: (0, 0)))
        args.append(g.reshape(1, -1).astype(F32))
        scratch.append(pltpu.VMEM((tm, ws[0].shape[0]), BF16))
    if res is not None:
        in_specs.append(pl.BlockSpec((tm, tn), lambda i, j: (i, j)))
        args.append(res)
    return pl.pallas_call(
        functools.partial(_mm_kernel, n_pairs=n_pairs, has_norm=g is not None, has_res=res is not None),
        grid=(M // tm, N // tn),
        in_specs=in_specs,
        out_specs=pl.BlockSpec((tm, tn), lambda i, j: (i, j)),
        out_shape=jax.ShapeDtypeStruct((M, N), out_dtype),
        scratch_shapes=scratch,
        compiler_params=_cp(("parallel", "arbitrary")),
        name=name,
    )(*args)


def _softplus(z):
    return jnp.maximum(z, 0.0) + jnp.log1p(jnp.exp(-jnp.abs(z)))


def _rwkv_kernel(*refs, C, HB, has_state):
    (r_ref, k_ref, v_ref, l_ref, pr_ref, pk_ref, pv_ref, plo_ref,
     mur_ref, muk_ref, muv_ref, mul_ref,
     w0_ref, a0_ref, kkw_ref, kaw_ref, rkw_ref, gg_ref, gb_ref,
     ww_ref, wa_ref, wg_ref) = refs[:22]
    idx = 22
    s0_ref = None
    if has_state:
        s0_ref = refs[idx]
        idx += 1
    o_ref, st_ref, prow_r, prow_k, prow_v, prow_l = refs[idx:idx + 6]
    c = pl.program_id(2)

    @pl.when(c == 0)
    def _():
        if has_state:
            st_ref[...] = s0_ref[...]
        else:
            st_ref[...] = jnp.zeros(st_ref.shape, F32)
        prow_r[...] = pr_ref[0]
        prow_k[...] = pk_ref[0]
        prow_v[...] = pv_ref[0]
        prow_l[...] = plo_ref[0]

    def shift_mix(x_ref, prow, mu_ref):
        x = x_ref[...]
        sh = pltpu.roll(x, 1, axis=0)
        row = lax.broadcasted_iota(jnp.int32, x.shape, 0)
        sh = jnp.where(row == 0, prow[...], sh)
        prow[...] = x[C - 1:C, :]
        return x + (sh - x) * mu_ref[...]

    r = shift_mix(r_ref, prow_r, mur_ref)
    k = shift_mix(k_ref, prow_k, muk_ref)
    v = shift_mix(v_ref, prow_v, muv_ref)
    lo = shift_mix(l_ref, prow_l, mul_ref)

    lw = _nn(jnp.tanh(lo).astype(BF16), ww_ref[...])
    la = _nn(lo.astype(BF16), wa_ref[...])
    g = _nn(jax.nn.sigmoid(lo).astype(BF16), wg_ref[...])
    w_log = -_softplus(-(w0_ref[...] + lw)) - 0.5
    ld = -jnp.exp(w_log)
    a = jax.nn.sigmoid(a0_ref[...] + la)

    ti = lax.broadcasted_iota(jnp.int32, (C, C), 0)
    tj = lax.broadcasted_iota(jnp.int32, (C, C), 1)
    tri = (ti >= tj).astype(F32)
    cum = jnp.dot(tri, ld, precision=HIGHEST, preferred_element_type=F32)
    eye = (ti == tj).astype(F32)
    strict = ti > tj
    incl = ti >= tj

    n_sq = max(C.bit_length() - 2, 0)
    H = range(HB)
    sls = [slice(h * HEAD, (h + 1) * HEAD) for h in H]

    ar, bt, kt, vb, bhat, khat, kfs, dec_end, S = [], [], [], [], [], [], [], [], []
    for h in H:
        sl = sls[h]
        kh, ah, ldh, cumh = k[:, sl], a[:, sl], ld[:, sl], cum[:, sl]
        kk = kh * kkw_ref[:, sl]
        kk = kk / jnp.maximum(jnp.sqrt(jnp.sum(kk * kk, axis=-1, keepdims=True)), 1e-12)
        kf = kh * (1.0 + (ah - 1.0) * kaw_ref[:, sl])
        bh = kk * ah
        cum_end = cumh[C - 1:C, :]
        e_neg = jnp.exp(-cumh)
        e_end = jnp.exp(cum_end - cumh)
        at = (-kk) * jnp.exp(cumh - ldh)
        rt = r[:, sl] * jnp.exp(cumh)
        ar.append(jnp.concatenate([at, rt], axis=0).astype(BF16))
        bt.append((bh * e_neg).astype(BF16))
        kt.append((kf * e_neg).astype(BF16))
        vb.append(v[:, sl].astype(BF16))
        bhat.append((bh * e_end).astype(BF16))
        khat.append((kf * e_end).astype(BF16))
        kfs.append(kf)
        dec_end.append(jnp.exp(cum_end))
        S.append(st_ref[0, h])
    mb = [_nt(ar[h], bt[h]) for h in H]
    mk = [_nt(ar[h], kt[h]) for h in H]
    w1 = [_nt(ar[h], S[h].astype(BF16)) for h in H]
    x = [jnp.where(strict, mb[h][:C], 0.0) for h in H]
    tinv = [eye + x[h] for h in H]
    for _ in range(n_sq):
        xb = [x[h].astype(BF16) for h in H]
        x = [_nn(xb[h], xb[h]) for h in H]
        tinv = [tinv[h] + _nn(tinv[h].astype(BF16), x[h].astype(BF16)) for h in H]
    rhs = [w1[h][:C] + _nn(jnp.where(strict, mk[h][:C], 0.0).astype(BF16), vb[h]) for h in H]
    ub = [_nn(tinv[h].astype(BF16), rhs[h].astype(BF16)).astype(BF16) for h in H]
    y = [w1[h][C:] + _nn(jnp.where(incl, mb[h][C:], 0.0).astype(BF16), ub[h])
         + _nn(jnp.where(incl, mk[h][C:], 0.0).astype(BF16), vb[h]) for h in H]
    for h in H:
        st_ref[0, h] = S[h] * dec_end[h] + _tn(ub[h], bhat[h]) + _tn(vb[h], khat[h])
    for h in H:
        sl = sls[h]
        mean = jnp.mean(y[h], axis=-1, keepdims=True)
        d = y[h] - mean
        var = jnp.mean(d * d, axis=-1, keepdims=True)
        yn = (d * lax.rsqrt(var + GN_EPS)) * gg_ref[:, sl] + gb_ref[:, sl]
        bonus = jnp.sum(r[:, sl] * kfs[h] * rkw_ref[:, sl], axis=-1, keepdims=True) * v[:, sl]
        o_ref[:, sl] = ((yn + bonus) * g[:, sl]).astype(o_ref.dtype)


def _rwkv(proj, prev, state0, B, S, C, HB, pp, out_dtype):
    HW = HB * HEAD
    nc = S // C
    nhg = RWKV_HEADS // HB
    kb = RWKV_WIDTH // HW
    lcol = COL_LORA // LORA_PAD

    def row(b, hg, c):
        return b * nc + c

    in_specs = [
        pl.BlockSpec((C, HW), lambda b, hg, c: (row(b, hg, c), hg)),
        pl.BlockSpec((C, HW), lambda b, hg, c: (row(b, hg, c), kb + hg)),
        pl.BlockSpec((C, HW), lambda b, hg, c: (row(b, hg, c), 2 * kb + hg)),
        pl.BlockSpec((C, LORA_PAD), lambda b, hg, c: (row(b, hg, c), lcol)),
        pl.BlockSpec((1, 1, HW), lambda b, hg, c: (b, 0, hg)),
        pl.BlockSpec((1, 1, HW), lambda b, hg, c: (b, 0, kb + hg)),
        pl.BlockSpec((1, 1, HW), lambda b, hg, c: (b, 0, 2 * kb + hg)),
        pl.BlockSpec((1, 1, LORA_PAD), lambda b, hg, c: (b, 0, lcol)),
        pl.BlockSpec((1, HW), lambda b, hg, c: (0, hg)),
        pl.BlockSpec((1, HW), lambda b, hg, c: (0, kb + hg)),
        pl.BlockSpec((1, HW), lambda b, hg, c: (0, 2 * kb + hg)),
        pl.BlockSpec((1, LORA_PAD), lambda b, hg, c: (0, lcol)),
    ]
    args = [proj, proj, proj, proj, prev, prev, prev, prev, pp['mu'], pp['mu'], pp['mu'], pp['mu']]
    for name in ('w0', 'a0', 'k_k', 'k_a', 'r_k', 'gn_g', 'gn_b'):
        in_specs.append(pl.BlockSpec((1, HW), lambda b, hg, c: (0, hg)))
        args.append(pp[name])
    for name in ('w_w2', 'w_a2', 'w_g2'):
        in_specs.append(pl.BlockSpec((LORA_PAD, HW), lambda b, hg, c: (0, hg)))
        args.append(pp[name])
    if state0 is not None:
        in_specs.append(pl.BlockSpec((1, HB, HEAD, HEAD), lambda b, hg, c: (b, hg, 0, 0)))
        args.append(state0)
    return pl.pallas_call(
        functools.partial(_rwkv_kernel, C=C, HB=HB, has_state=state0 is not None),
        grid=(B, nhg, nc),
        in_specs=in_specs,
        out_specs=[
            pl.BlockSpec((C, HW), lambda b, hg, c: (row(b, hg, c), hg)),
            pl.BlockSpec((1, HB, HEAD, HEAD), lambda b, hg, c: (b, hg, 0, 0)),
        ],
        out_shape=[
            jax.ShapeDtypeStruct((B * S, RWKV_WIDTH), out_dtype),
            jax.ShapeDtypeStruct((B, RWKV_HEADS, HEAD, HEAD), F32),
        ],
        scratch_shapes=[pltpu.VMEM((1, HW), F32), pltpu.VMEM((1, HW), F32), pltpu.VMEM((1, HW), F32),
                        pltpu.VMEM((1, LORA_PAD), F32)],
        compiler_params=_cp(("parallel", "parallel", "arbitrary")),
        name="rwkv7_mix",
    )(*args)


def _swap_halves(x):
    half = x.shape[-1] // 2
    return jnp.concatenate([x[:, half:], x[:, :half]], axis=1)


def _kvprep_kernel(kv_ref, g_ref, cos_ref, sin_ref, c_ref, kp_ref, cb_ref, kpb_ref):
    kv = kv_ref[...]
    lat = kv[:, :KV_LORA]
    ms = jnp.mean(lat * lat, axis=-1, keepdims=True)
    c = (lat * lax.rsqrt(ms + NORM_EPS)) * g_ref[...]
    xr = kv[:, KV_LORA:KV_LORA + ROPE]
    kp = xr * cos_ref[...] + _swap_halves(xr) * sin_ref[...]
    c_ref[...] = c
    kp_ref[...] = kp
    cb_ref[...] = c.astype(BF16)
    kpb_ref[...] = kp.astype(BF16)


def _kvprep(proj, g, cos64, sin64, tm):
    N = proj.shape[0]
    nper = cos64.shape[0] // tm
    return pl.pallas_call(
        _kvprep_kernel,
        grid=(N // tm,),
        in_specs=[
            pl.BlockSpec((tm, KV_PAD), lambda i: (i, COL_KV // KV_PAD)),
            pl.BlockSpec((1, KV_LORA), lambda i: (0, 0)),
            pl.BlockSpec((tm, ROPE), lambda i: (i % nper, 0)),
            pl.BlockSpec((tm, ROPE), lambda i: (i % nper, 0)),
        ],
        out_specs=[
            pl.BlockSpec((tm, KV_LORA), lambda i: (i, 0)),
            pl.BlockSpec((tm, ROPE), lambda i: (i, 0)),
            pl.BlockSpec((tm, KV_LORA), lambda i: (i, 0)),
            pl.BlockSpec((tm, ROPE), lambda i: (i, 0)),
        ],
        out_shape=[
            jax.ShapeDtypeStruct((N, KV_LORA), F32),
            jax.ShapeDtypeStruct((N, ROPE), F32),
            jax.ShapeDtypeStruct((N, KV_LORA), BF16),
            jax.ShapeDtypeStruct((N, ROPE), BF16),
        ],
        compiler_params=_cp(("parallel",)),
        name="mla_kv_prep",
    )(proj, g.reshape(1, -1), cos64, sin64)


def _prep_queries(qn_ref, qr_ref, cos_ref, sin_ref, wuk_ref, qlat_s, qpe_s, tq):
    qn = qn_ref[...]
    for h in range(MLA_HEADS):
        ql = _nn(qn[:, h * NOPE:(h + 1) * NOPE].astype(BF16), wuk_ref[h])
        qlat_s[h * tq:(h + 1) * tq, :] = (ql * MLA_SCALE).astype(BF16)
    qr = qr_ref[...]
    width = qr.shape[1]
    lane = lax.broadcasted_iota(jnp.int32, qr.shape, 1)
    half = ROPE // 2
    rot = jnp.where(lane % ROPE < half, pltpu.roll(qr, width - half, axis=1), pltpu.roll(qr, half, axis=1))
    qp = (qr * cos_ref[...] + rot * sin_ref[...]) * MLA_SCALE
    for h in range(MLA_HEADS):
        qpe_s[h * tq:(h + 1) * tq, :] = qp[:, h * ROPE:(h + 1) * ROPE].astype(BF16)


def _online_softmax_step(s, kcb, m_s, l_s, acc_s):
    m_old = m_s[...]
    m_new = jnp.maximum(m_old, jnp.max(s, axis=-1, keepdims=True))
    alpha = jnp.exp(m_old - m_new)
    p = jnp.exp(s - m_new)
    l_s[...] = alpha * l_s[...] + jnp.sum(p, axis=-1, keepdims=True)
    acc_s[...] = alpha * acc_s[...] + _nn(p.astype(BF16), kcb)
    m_s[...] = m_new


def _finish_attention(o_ref, wuv_ref, l_s, acc_s, tq):
    for h in range(MLA_HEADS):
        rows = slice(h * tq, (h + 1) * tq)
        o_lat = acc_s[rows, :] / l_s[rows, :]
        o_ref[:, h * V_DIM:(h + 1) * V_DIM] = _nn(o_lat.astype(BF16), wuv_ref[h]).astype(o_ref.dtype)


def _attn_prompt_kernel(qn_ref, qr_ref, cos_ref, sin_ref, kc_ref, kp_ref, wuk_ref, wuv_ref, o_ref,
                        qlat_s, qpe_s, m_s, l_s, acc_s, *, tq, tk):
    i = pl.program_id(1)
    _prep_queries(qn_ref, qr_ref, cos_ref, sin_ref, wuk_ref, qlat_s, qpe_s, tq)
    m_s[...] = jnp.full(m_s.shape, -jnp.inf, F32)
    l_s[...] = jnp.zeros(l_s.shape, F32)
    acc_s[...] = jnp.zeros(acc_s.shape, F32)
    rows = MLA_HEADS * tq
    qpos = i * tq + lax.broadcasted_iota(jnp.int32, (rows, tk), 0) % tq
    kidx = lax.broadcasted_iota(jnp.int32, (rows, tk), 1)

    def body(j, carry):
        start = pl.multiple_of(j * tk, tk)
        kcb = kc_ref[pl.ds(start, tk), :]
        s = _nt(qlat_s[...], kcb) + _nt(qpe_s[...], kp_ref[pl.ds(start, tk), :])
        s = jnp.where(kidx + j * tk <= qpos, s, -jnp.inf)
        _online_softmax_step(s, kcb, m_s, l_s, acc_s)
        return carry

    lax.fori_loop(0, (i * tq + tq + tk - 1) // tk, body, 0)
    _finish_attention(o_ref, wuv_ref, l_s, acc_s, tq)


def _attn_prompt(q, kcb, kpb, cosq, sinq, wuk, wuv, B, S, tq, tk):
    nq = S // tq
    rows = MLA_HEADS * tq
    return pl.pallas_call(
        functools.partial(_attn_prompt_kernel, tq=tq, tk=tk),
        grid=(B, nq),
        in_specs=[
            pl.BlockSpec((tq, MLA_HEADS * NOPE), lambda b, i: (b * nq + i, 0)),
            pl.BlockSpec((tq, MLA_HEADS * ROPE), lambda b, i: (b * nq + i, (MLA_HEADS * NOPE) // (MLA_HEADS * ROPE))),
            pl.BlockSpec((tq, MLA_HEADS * ROPE), lambda b, i: (i, 0)),
            pl.BlockSpec((tq, MLA_HEADS * ROPE), lambda b, i: (i, 0)),
            pl.BlockSpec((S, KV_LORA), lambda b, i: (b, 0)),
            pl.BlockSpec((S, ROPE), lambda b, i: (b, 0)),
            pl.BlockSpec((MLA_HEADS, NOPE, KV_LORA), lambda b, i: (0, 0, 0)),
            pl.BlockSpec((MLA_HEADS, KV_LORA, V_DIM), lambda b, i: (0, 0, 0)),
        ],
        out_specs=pl.BlockSpec((tq, MLA_HEADS * V_DIM), lambda b, i: (b * nq + i, 0)),
        out_shape=jax.ShapeDtypeStruct((B * S, MLA_HEADS * V_DIM), BF16),
        scratch_shapes=[
            pltpu.VMEM((rows, KV_LORA), BF16), pltpu.VMEM((rows, ROPE), BF16),
            pltpu.VMEM((rows, 1), F32), pltpu.VMEM((rows, 1), F32), pltpu.VMEM((rows, KV_LORA), F32),
        ],
        compiler_params=_cp(("parallel", "arbitrary")),
        name="mla_attn_prompt",
    )(q, q, cosq, sinq, kcb, kpb, wuk, wuv)


def _attn_sample_kernel(pt_ref, qn_ref, qr_ref, cos_ref, sin_ref, cn_ref, kn_ref, wuk_ref, wuv_ref,
                        ckv_hbm, ckr_hbm, o_ref,
                        qlat_s, qpe_s, m_s, l_s, acc_s, cbuf, rbuf, nc_s, nk_s, sem_c, sem_r,
                        *, T, G, n_pages):
    b = pl.program_id(0)
    nb = pl.num_programs(0)
    ngrp = n_pages // G
    rows = MLA_HEADS * T

    def page_copies(seq, grp, slot, p):
        page = pt_ref[seq, grp * G + p]
        cc = pltpu.make_async_copy(ckv_hbm.at[page], cbuf.at[slot, pl.ds(p * PAGE, PAGE)], sem_c.at[slot])
        cr = pltpu.make_async_copy(ckr_hbm.at[page], rbuf.at[slot, :, pl.ds(p * PAGE, PAGE)], sem_r.at[slot])
        return cc, cr

    def start_group(seq, grp, slot):
        for p in range(G):
            cc, cr = page_copies(seq, grp, slot, p)
            cc.start()
            cr.start()

    def wait_group(seq, grp, slot):
        for p in range(G):
            cc, cr = page_copies(seq, grp, slot, p)
            cc.wait()
            cr.wait()

    @pl.when(b == 0)
    def _():
        start_group(0, 0, 0)

    _prep_queries(qn_ref, qr_ref, cos_ref, sin_ref, wuk_ref, qlat_s, qpe_s, T)
    m_s[...] = jnp.full(m_s.shape, -jnp.inf, F32)
    l_s[...] = jnp.zeros(l_s.shape, F32)
    acc_s[...] = jnp.zeros(acc_s.shape, F32)

    def body(grp, carry):
        gg = b * ngrp + grp
        slot = gg % 2
        wait_group(b, grp, slot)

        @pl.when(grp + 1 < ngrp)
        def _():
            start_group(b, grp + 1, 1 - slot)

        @pl.when(jnp.logical_and(grp + 1 == ngrp, b + 1 < nb))
        def _():
            start_group(b + 1, 0, 1 - slot)

        kcb = cbuf[slot].astype(BF16)
        s = _nt(qlat_s[...], kcb) + _nn(qpe_s[...], rbuf[slot].astype(BF16))
        _online_softmax_step(s, kcb, m_s, l_s, acc_s)
        return carry

    lax.fori_loop(0, ngrp, body, 0)

    nc_s[...] = jnp.zeros(nc_s.shape, F32)
    nk_s[...] = jnp.zeros(nk_s.shape, F32)
    nc_s[0:T, :] = cn_ref[...]
    nk_s[0:T, :] = kn_ref[...]
    kcb = nc_s[...].astype(BF16)
    s = _nt(qlat_s[...], kcb) + _nt(qpe_s[...], nk_s[...].astype(BF16))
    qt = lax.broadcasted_iota(jnp.int32, (rows, PAGE), 0) % T
    kidx = lax.broadcasted_iota(jnp.int32, (rows, PAGE), 1)
    s = jnp.where(kidx <= qt, s, -jnp.inf)
    _online_softmax_step(s, kcb, m_s, l_s, acc_s)
    _finish_attention(o_ref, wuv_ref, l_s, acc_s, T)


def _attn_sample(q, c_new, k_new, cosq, sinq, wuk, wuv, page_table, cache_kv, cache_kr, DB, T, G):
    n_pages = page_table.shape[1]
    rows = MLA_HEADS * T
    grid_spec = pltpu.PrefetchScalarGridSpec(
        num_scalar_prefetch=1,
        grid=(DB,),
        in_specs=[
            pl.BlockSpec((T, MLA_HEADS * NOPE), lambda b, pt: (b, 0)),
            pl.BlockSpec((T, MLA_HEADS * ROPE), lambda b, pt: (b, (MLA_HEADS * NOPE) // (MLA_HEADS * ROPE))),
            pl.BlockSpec((T, MLA_HEADS * ROPE), lambda b, pt: (0, 0)),
            pl.BlockSpec((T, MLA_HEADS * ROPE), lambda b, pt: (0, 0)),
            pl.BlockSpec((T, KV_LORA), lambda b, pt: (b, 0)),
            pl.BlockSpec((T, ROPE), lambda b, pt: (b, 0)),
            pl.BlockSpec((MLA_HEADS, NOPE, KV_LORA), lambda b, pt: (0, 0, 0)),
            pl.BlockSpec((MLA_HEADS, KV_LORA, V_DIM), lambda b, pt: (0, 0, 0)),
            pl.BlockSpec(memory_space=pl.ANY),
            pl.BlockSpec(memory_space=pl.ANY),
        ],
        out_specs=pl.BlockSpec((T, MLA_HEADS * V_DIM), lambda b, pt: (b, 0)),
        scratch_shapes=[
            pltpu.VMEM((rows, KV_LORA), BF16), pltpu.VMEM((rows, ROPE), BF16),
            pltpu.VMEM((rows, 1), F32), pltpu.VMEM((rows, 1), F32), pltpu.VMEM((rows, KV_LORA), F32),
            pltpu.VMEM((2, G * PAGE, KV_LORA), F32), pltpu.VMEM((2, ROPE, G * PAGE), F32),
            pltpu.VMEM((PAGE, KV_LORA), F32), pltpu.VMEM((PAGE, ROPE), F32),
            pltpu.SemaphoreType.DMA((2,)), pltpu.SemaphoreType.DMA((2,)),
        ],
    )
    return pl.pallas_call(
        functools.partial(_attn_sample_kernel, T=T, G=G, n_pages=n_pages),
        grid_spec=grid_spec,
        out_shape=jax.ShapeDtypeStruct((DB * T, MLA_HEADS * V_DIM), F32),
        compiler_params=_cp(("arbitrary",)),
        name="mla_attn_sample",
    )(page_table, q, q, cosq, sinq, c_new, k_new, wuk, wuv, cache_kv, cache_kr)


ROW_TILES = D_MODEL // 128


def _store_rows(ref, x):
    for s in range(ROW_TILES):
        ref[:, s, :] = x[:, s * 128:(s + 1) * 128]


def _load_rows(ref, slot, rows=slice(None)):
    return jnp.concatenate([ref[slot, rows, s, :] for s in range(ROW_TILES)], axis=1)


def _memblock_kernel(x_ref, mk_ref, mv_ref, gm_ref, wq_ref, wo_ref, ge_ref, wr_ref, br_ref,
                     x2_ref, hn_ref, eid_ref, wts_ref, o_s, *, nb, tq):
    x = x_ref[...]
    ms = jnp.mean(x * x, axis=-1, keepdims=True)
    xn = ((x * lax.rsqrt(ms + NORM_EPS)) * gm_ref[...]).astype(BF16)
    q = _nn(xn, wq_ref[...]) * MEM_SCALE
    for n in range(nb):
        rows = slice(n * tq, (n + 1) * tq)
        for h in range(MEM_HEADS):
            cols = slice(h * MEM_HEAD_DIM, (h + 1) * MEM_HEAD_DIM)
            s = _nt(q[rows, cols].astype(BF16), mk_ref[n, :, cols].astype(BF16))
            m = jnp.max(s, axis=-1, keepdims=True)
            p = jnp.exp(s - m)
            p = p / jnp.sum(p, axis=-1, keepdims=True)
            o_s[rows, cols] = _nn(p.astype(BF16), mv_ref[n, :, cols].astype(BF16))
    x2 = x + _nn(o_s[...].astype(BF16), wo_ref[...])
    x2_ref[...] = x2
    ms2 = jnp.mean(x2 * x2, axis=-1, keepdims=True)
    hn = (x2 * lax.rsqrt(ms2 + NORM_EPS)) * ge_ref[...]
    _store_rows(hn_ref, hn)

    logits = jnp.dot(hn, wr_ref[...], precision=HIGHEST, preferred_element_type=F32) + br_ref[...]
    lane = lax.broadcasted_iota(jnp.int32, logits.shape, 1)
    big = jnp.int32(ROUTER_PAD)
    lc = jnp.where(lane < N_GROUPS, logits, -jnp.inf)
    mc = jnp.max(lc, axis=-1, keepdims=True)
    grp = jnp.min(jnp.where(lc == mc, lane, big), axis=-1, keepdims=True)
    gate_c = 1.0 / jnp.sum(jnp.exp(lc - mc), axis=-1, keepdims=True)
    lo = N_GROUPS + EPG * grp
    fmask = jnp.logical_and(lane >= lo, lane < lo + EPG)
    lf = jnp.where(fmask, logits, -jnp.inf)
    mf = jnp.max(lf, axis=-1, keepdims=True)
    ef = jnp.exp(lf - mf)
    pf = ef / jnp.sum(ef, axis=-1, keepdims=True)
    pf = jnp.where(fmask, pf, -1.0)
    v1 = jnp.max(pf, axis=-1, keepdims=True)
    i1 = jnp.min(jnp.where(pf == v1, lane, big), axis=-1, keepdims=True)
    pf2 = jnp.where(lane == i1, -1.0, pf)
    v2 = jnp.max(pf2, axis=-1, keepdims=True)
    i2 = jnp.min(jnp.where(pf2 == v2, lane, big), axis=-1, keepdims=True)
    den = v1 + v2
    eid_ref[...] = jnp.where(lane == 0, i1 - N_GROUPS, jnp.where(lane == 1, i2 - N_GROUPS, 0))
    wts_ref[...] = jnp.where(lane == 0, gate_c * v1 / den, jnp.where(lane == 1, gate_c * v2 / den, 0.0))


def _memblock(x, mem_k, mem_v, lw, nb, tq, tiles_per_mem):
    N = x.shape[0]
    R = nb * tq
    return pl.pallas_call(
        functools.partial(_memblock_kernel, nb=nb, tq=tq),
        grid=(N // R,),
        in_specs=[
            pl.BlockSpec((R, D_MODEL), lambda i: (i, 0)),
            pl.BlockSpec((nb, MEM_TOKENS, MEM_WIDTH), lambda i: (i // tiles_per_mem, 0, 0)),
            pl.BlockSpec((nb, MEM_TOKENS, MEM_WIDTH), lambda i: (i // tiles_per_mem, 0, 0)),
            pl.BlockSpec((1, D_MODEL), lambda i: (0, 0)),
            pl.BlockSpec((D_MODEL, MEM_WIDTH), lambda i: (0, 0)),
            pl.BlockSpec((MEM_WIDTH, D_MODEL), lambda i: (0, 0)),
            pl.BlockSpec((1, D_MODEL), lambda i: (0, 0)),
            pl.BlockSpec((D_MODEL, ROUTER_PAD), lambda i: (0, 0)),
            pl.BlockSpec((1, ROUTER_PAD), lambda i: (0, 0)),
        ],
        out_specs=[
            pl.BlockSpec((R, D_MODEL), lambda i: (i, 0)),
            pl.BlockSpec((R, ROW_TILES, 128), lambda i: (i, 0, 0)),
            pl.BlockSpec((R, ROUTER_PAD), lambda i: (i, 0)),
            pl.BlockSpec((R, ROUTER_PAD), lambda i: (i, 0)),
        ],
        out_shape=[
            jax.ShapeDtypeStruct((N, D_MODEL), F32),
            jax.ShapeDtypeStruct((N, ROW_TILES, 128), F32),
            jax.ShapeDtypeStruct((N, ROUTER_PAD), jnp.int32),
            jax.ShapeDtypeStruct((N, ROUTER_PAD), F32),
        ],
        scratch_shapes=[pltpu.VMEM((R, MEM_WIDTH), F32)],
        compiler_params=_cp(("parallel",)),
        name="mem_attn_router",
    )(x, mem_k, mem_v, lw['ln_mem_g'], lw['w_mq'], lw['w_mo'], lw['ln_moe_g'], lw['w_router'], lw['b_router'])


def _moe_kernel(be_ref, nu_ref, tok_ref, hn_hbm, wg_ref, wu_ref, wd_ref, y_ref,
                xbuf, wgb, wub, wdb, sem):
    i = pl.program_id(0)
    nused = nu_ref[0]

    def row_copy(which, slot, r):
        return pltpu.make_async_copy(hn_hbm.at[tok_ref[0, which, r]], xbuf.at[slot, r], sem.at[slot])

    def start_rows(which, slot):
        def body(r, carry):
            row_copy(which, slot, r).start()
            return carry
        lax.fori_loop(0, MOE_BLOCK, body, 0)

    def wait_rows(which, slot):
        def body(r, carry):
            row_copy(which, slot, r).wait()
            return carry
        lax.fori_loop(0, MOE_BLOCK, body, 0)

    slot = i % 2

    @pl.when(jnp.logical_and(i == 0, nused > 0))
    def _():
        start_rows(0, 0)

    @pl.when(i < nused)
    def _():
        wait_rows(0, slot)

        @pl.when(i + 1 < nused)
        def _():
            start_rows(1, 1 - slot)

        changed = jnp.logical_or(i == 0, be_ref[i] != be_ref[jnp.maximum(i - 1, 0)])

        @pl.when(changed)
        def _():
            wgb[...] = wg_ref[0].astype(BF16)
            wub[...] = wu_ref[0].astype(BF16)
            wdb[...] = wd_ref[0].astype(BF16)

        xb = _load_rows(xbuf, slot).astype(BF16)
        gate = _nn(xb, wgb[...])
        up = _nn(xb, wub[...])
        hid = (gate * jax.nn.sigmoid(gate)) * up
        _store_rows(y_ref, _nn(hid.astype(BF16), wdb[...]))

    @pl.when(i >= nused)
    def _():
        y_ref[...] = jnp.zeros(y_ref.shape, F32)


def _moe_experts(hn, slot_tok, block_e, nused, w_gate, w_up, w_down):
    n_blocks = block_e.shape[0]
    tok2 = jnp.stack([slot_tok, jnp.roll(slot_tok, -1, axis=0)], axis=1)
    grid_spec = pltpu.PrefetchScalarGridSpec(
        num_scalar_prefetch=2,
        grid=(n_blocks,),
        in_specs=[
            pl.BlockSpec((1, 2, MOE_BLOCK), lambda i, be, nu: (i, 0, 0), memory_space=pltpu.SMEM),
            pl.BlockSpec(memory_space=pl.ANY),
            pl.BlockSpec((1, D_MODEL, EXPERT_FF), lambda i, be, nu: (be[i], 0, 0)),
            pl.BlockSpec((1, D_MODEL, EXPERT_FF), lambda i, be, nu: (be[i], 0, 0)),
            pl.BlockSpec((1, EXPERT_FF, D_MODEL), lambda i, be, nu: (be[i], 0, 0)),
        ],
        out_specs=pl.BlockSpec((MOE_BLOCK, ROW_TILES, 128), lambda i, be, nu: (i, 0, 0)),
        scratch_shapes=[
            pltpu.VMEM((2, MOE_BLOCK, ROW_TILES, 128), F32),
            pltpu.VMEM((D_MODEL, EXPERT_FF), BF16), pltpu.VMEM((D_MODEL, EXPERT_FF), BF16),
            pltpu.VMEM((EXPERT_FF, D_MODEL), BF16),
            pltpu.SemaphoreType.DMA((2,)),
        ],
    )
    return pl.pallas_call(
        _moe_kernel,
        grid_spec=grid_spec,
        out_shape=jax.ShapeDtypeStruct((n_blocks * MOE_BLOCK, ROW_TILES, 128), F32),
        compiler_params=_cp(("arbitrary",)),
        name="moe_experts",
    )(block_e, nused, tok2, hn, w_gate, w_up, w_down)


def _combine_kernel(pos_ref, x_ref, w_ref, g_ref, yb_hbm, o_ref, rbuf, sem, *, R):
    i = pl.program_id(0)
    n = pl.num_programs(0)

    def row_copy(which, slot, r):
        return pltpu.make_async_copy(yb_hbm.at[pos_ref[0, which, r]], rbuf.at[slot, r], sem.at[slot])

    def start_rows(which, slot):
        def body(r, carry):
            row_copy(which, slot, r).start()
            return carry
        lax.fori_loop(0, 2 * R, body, 0)

    def wait_rows(which, slot):
        def body(r, carry):
            row_copy(which, slot, r).wait()
            return carry
        lax.fori_loop(0, 2 * R, body, 0)

    slot = i % 2

    @pl.when(i == 0)
    def _():
        start_rows(0, 0)

    wait_rows(0, slot)

    @pl.when(i + 1 < n)
    def _():
        start_rows(1, 1 - slot)

    w = w_ref[...]
    x = x_ref[...] + (w[:, 0:1] * _load_rows(rbuf, slot, slice(0, R))
                      + w[:, 1:2] * _load_rows(rbuf, slot, slice(R, 2 * R)))
    ms = jnp.mean(x * x, axis=-1, keepdims=True)
    o_ref[...] = (x * lax.rsqrt(ms + NORM_EPS)) * g_ref[...]


def _moe_combine(x2, wts, pos, yb, final_g, R):
    N = x2.shape[0]
    nt = N // R
    p = pos.reshape(nt, R, 2).transpose(0, 2, 1).reshape(nt, 2 * R)
    p2 = jnp.stack([p, jnp.roll(p, -1, axis=0)], axis=1)
    return pl.pallas_call(
        functools.partial(_combine_kernel, R=R),
        grid=(nt,),
        in_specs=[
            pl.BlockSpec((1, 2, 2 * R), lambda i: (i, 0, 0), memory_space=pltpu.SMEM),
            pl.BlockSpec((R, D_MODEL), lambda i: (i, 0)),
            pl.BlockSpec((R, ROUTER_PAD), lambda i: (i, 0)),
            pl.BlockSpec((1, D_MODEL), lambda i: (0, 0)),
            pl.BlockSpec(memory_space=pl.ANY),
        ],
        out_specs=pl.BlockSpec((R, D_MODEL), lambda i: (i, 0)),
        out_shape=jax.ShapeDtypeStruct((N, D_MODEL), F32),
        scratch_shapes=[pltpu.VMEM((2, 2 * R, ROW_TILES, 128), F32), pltpu.SemaphoreType.DMA((2,))],
        compiler_params=_cp(("arbitrary",)),
        name="moe_combine_norm",
    )(p2, x2, wts, final_g.reshape(1, -1), yb)


def _moe_dispatch_tables(eid, n_tokens):
    A = n_tokens * 2
    flat_e = eid.reshape(-1)
    flat_tok = jnp.arange(A, dtype=jnp.int32) // 2
    onehot = (flat_e[:, None] == jnp.arange(N_EXPERTS, dtype=jnp.int32)[None, :]).astype(jnp.int32)
    ranks = jnp.cumsum(onehot, axis=0)
    counts = ranks[-1]
    rank = jnp.take_along_axis(ranks, flat_e[:, None], axis=1)[:, 0] - 1
    padded = (counts + MOE_BLOCK - 1) // MOE_BLOCK * MOE_BLOCK
    pends = jnp.cumsum(padded)
    pstarts = pends - padded
    dest = pstarts[flat_e] + rank
    n_blocks = -(-A // MOE_BLOCK) + N_EXPERTS
    n_slots = n_blocks * MOE_BLOCK
    slot_tok = jnp.zeros((n_slots,), jnp.int32).at[dest].set(flat_tok)
    block_start = jnp.arange(n_blocks, dtype=jnp.int32) * MOE_BLOCK
    block_e = jnp.minimum(jnp.searchsorted(pends, block_start, side='right'), N_EXPERTS - 1).astype(jnp.int32)
    nused = (pends[-1:] // MOE_BLOCK).astype(jnp.int32)
    return slot_tok.reshape(n_blocks, MOE_BLOCK), block_e, nused, dest.reshape(n_tokens, 2)


def _rope_tables(pos, reps):
    half = ROPE // 2
    inv = ROPE_THETA ** (-jnp.arange(half, dtype=F32) / half)
    ang = pos.astype(F32)[:, None] * inv[None, :]
    cos, sin = jnp.cos(ang), jnp.sin(ang)
    cos64 = jnp.concatenate([cos, cos], axis=1)
    sin64 = jnp.concatenate([-sin, sin], axis=1)
    return cos64, sin64, jnp.tile(cos64, (1, reps)), jnp.tile(sin64, (1, reps))


def _to_layout(v):
    out = jnp.zeros(v.shape[:-1] + (PROJ_W,), F32)
    out = out.at[..., COL_MAIN:COL_MAIN + 3 * RWKV_WIDTH].set(v[..., :3 * RWKV_WIDTH])
    return out.at[..., COL_LORA:COL_LORA + LORA_W].set(v[..., 3 * RWKV_WIDTH:])


def _from_layout(p):
    return jnp.concatenate([p[..., :3 * RWKV_WIDTH], p[..., COL_LORA:COL_LORA + LORA_W]], axis=-1)


def _pad_rows(w, start, total):
    return jnp.zeros((total, w.shape[1]), w.dtype).at[start:start + w.shape[0]].set(w)


def _prepare(ln_mix_g, w_in, rwkv_mu, rwkv_w0, rwkv_w_w2, rwkv_a0, rwkv_w_a2, rwkv_w_g2, rwkv_k_k, rwkv_k_a,
             rwkv_r_k, rwkv_gn_g, rwkv_gn_b, mla_q_norm_g, mla_w_qb, mla_kv_norm_g, mla_w_kvb, w_out, ln_mem_g,
             mem_norm_g, w_mq, w_mk, w_mv, w_mo, ln_moe_g, w_router_c, b_router_c, w_router_f, b_router_f):
    i3 = 3 * RWKV_WIDTH
    w_in_wide = jnp.zeros((D_MODEL, PROJ_W), F32)
    w_in_wide = w_in_wide.at[:, COL_MAIN:COL_MAIN + i3].set(w_in[:, :i3])
    w_in_wide = w_in_wide.at[:, COL_LORA:COL_LORA + LORA_W].set(w_in[:, i3:RWKV_PROJ])
    w_in_wide = w_in_wide.at[:, COL_QA:COL_QA + Q_LORA].set(w_in[:, RWKV_PROJ:RWKV_PROJ + Q_LORA])
    w_in_wide = w_in_wide.at[:, COL_KV:COL_KV + KV_LORA + ROPE].set(w_in[:, RWKV_PROJ + Q_LORA:])
    row = lambda v: v.reshape(1, -1).astype(F32)
    rw = {
        'mu': _to_layout(rwkv_mu.reshape(1, -1)),
        'w0': row(rwkv_w0), 'a0': row(rwkv_a0), 'k_k': row(rwkv_k_k), 'k_a': row(rwkv_k_a),
        'r_k': row(rwkv_r_k), 'gn_g': row(rwkv_gn_g), 'gn_b': row(rwkv_gn_b),
        'w_w2': _pad_rows(rwkv_w_w2, 0, LORA_PAD).astype(BF16),
        'w_a2': _pad_rows(rwkv_w_a2, W_LORA, LORA_PAD).astype(BF16),
        'w_g2': _pad_rows(rwkv_w_g2, W_LORA + A_LORA, LORA_PAD).astype(BF16),
    }
    qb = mla_w_qb.reshape(Q_LORA, MLA_HEADS, NOPE + ROPE)
    w_qb = jnp.concatenate([qb[:, :, :NOPE].reshape(Q_LORA, -1), qb[:, :, NOPE:].reshape(Q_LORA, -1)], axis=1)
    kvb = mla_w_kvb.reshape(KV_LORA, MLA_HEADS, NOPE + V_DIM)
    w_router = jnp.zeros((D_MODEL, ROUTER_PAD), F32)
    w_router = w_router.at[:, :N_GROUPS].set(w_router_c).at[:, N_GROUPS:N_GROUPS + N_EXPERTS].set(w_router_f)
    b_router = jnp.zeros((1, ROUTER_PAD), F32)
    b_router = b_router.at[0, :N_GROUPS].set(b_router_c).at[0, N_GROUPS:N_GROUPS + N_EXPERTS].set(b_router_f)
    return {
        'ln_mix_g': ln_mix_g, 'w_in': w_in_wide.astype(BF16), 'rwkv': rw,
        'q_norm_g': mla_q_norm_g, 'w_qb': w_qb.astype(BF16), 'kv_norm_g': mla_kv_norm_g,
        'w_uk': jnp.transpose(kvb[:, :, :NOPE], (1, 2, 0)).astype(BF16),
        'w_uv': jnp.transpose(kvb[:, :, NOPE:], (1, 0, 2)).astype(BF16),
        'w_out_r': w_out[:RWKV_WIDTH].astype(BF16), 'w_out_m': w_out[RWKV_WIDTH:].astype(BF16),
        'mem_norm_g': mem_norm_g, 'w_mkv': jnp.concatenate([w_mk, w_mv], axis=1).astype(BF16),
        'ln_mem_g': row(ln_mem_g), 'w_mq': w_mq.astype(BF16), 'w_mo': w_mo.astype(BF16),
        'ln_moe_g': row(ln_moe_g), 'w_router': w_router, 'b_router': b_router,
    }


def _decoder_group(x, B, S, prev, state0, mem_k, mem_v, lw, w_gate, w_up, w_down, final_g, cfg, attend):
    N = B * S
    proj = _mm([x], [lw['w_in']], g=lw['ln_mix_g'], tm=cfg['tm'], tn=512, name="in_proj")
    o_rwkv, st = _rwkv(proj, prev, state0, B, S, cfg['C'], cfg['HB'], lw['rwkv'], cfg['o_dtype'])
    c_kv, k_pe, ckb, kpb = _kvprep(proj, lw['kv_norm_g'], cfg['cos64'], cfg['sin64'], cfg['tkv'])
    q = _mm([proj], [lw['w_qb']], g=lw['q_norm_g'], tm=cfg['tm'], tn=512, x_cols=[COL_QA // Q_LORA], name="q_proj")
    o_mla = attend(q, c_kv, k_pe, ckb, kpb)
    x1 = _mm([o_rwkv, o_mla], [lw['w_out_r'], lw['w_out_m']], res=x, tm=cfg['tm'], tn=512, name="out_proj")
    x2, hn, eid, wts = _memblock(x1, mem_k, mem_v, lw, cfg['nb'], cfg['tq_mem'], cfg['tiles_per_mem'])
    slot_tok, block_e, nused, pos = _moe_dispatch_tables(eid[:, :2], N)
    yb = _moe_experts(hn, slot_tok, block_e, nused, w_gate, w_up, w_down)
    y = _moe_combine(x2, wts, pos, yb, final_g, min(MOE_BLOCK, N))
    shift = _from_layout(proj.reshape(B, S, PROJ_W)[:, -1])
    return y, c_kv, k_pe, st, shift


def kernel(x_prompt, x_sample, mem_prompt, cache_kv_latent, cache_k_rope, cache_mem_k, cache_mem_v, state_rwkv, state_rwkv_shift, page_table, ln_mix_g, w_in, rwkv_mu, rwkv_w0, rwkv_w_w2, rwkv_a0, rwkv_w_a2, rwkv_w_g2, rwkv_k_k, rwkv_k_a, rwkv_r_k, rwkv_gn_g, rwkv_gn_b, mla_q_norm_g, mla_w_qb, mla_kv_norm_g, mla_w_kvb, w_out, ln_mem_g, mem_norm_g, w_mq, w_mk, w_mv, w_mo, ln_moe_g, w_router_c, b_router_c, w_router_f, b_router_f, w_gate, w_up, w_down, final_norm_g):
    B, S, D = x_prompt.shape
    DB, T, _ = x_sample.shape
    assert w_in.shape[0] == 1, "single-layer step"
    past_len = page_table.shape[1] * PAGE
    lw = _prepare(ln_mix_g[0], w_in[0], rwkv_mu[0], rwkv_w0[0], rwkv_w_w2[0], rwkv_a0[0], rwkv_w_a2[0],
                  rwkv_w_g2[0], rwkv_k_k[0], rwkv_k_a[0], rwkv_r_k[0], rwkv_gn_g[0], rwkv_gn_b[0],
                  mla_q_norm_g[0], mla_w_qb[0], mla_kv_norm_g[0], mla_w_kvb[0], w_out[0], ln_mem_g[0],
                  mem_norm_g[0], w_mq[0], w_mk[0], w_mv[0], w_mo[0], ln_moe_g[0], w_router_c[0],
                  b_router_c[0], w_router_f[0], b_router_f[0])
    wg = w_gate.reshape(N_EXPERTS, D_MODEL, EXPERT_FF)
    wu = w_up.reshape(N_EXPERTS, D_MODEL, EXPERT_FF)
    wd = w_down.reshape(N_EXPERTS, EXPERT_FF, D_MODEL)
    n_phys = cache_kv_latent.shape[1]

    cos64_p, sin64_p, cosq_p, sinq_p = _rope_tables(jnp.arange(S), MLA_HEADS)
    mem_kv = _mm([mem_prompt.reshape(B * MEM_TOKENS, D)], [lw['w_mkv']], g=lw['mem_norm_g'],
                 tm=512, tn=512, name="mem_kv_proj")
    mem_k = mem_kv[:, :MEM_WIDTH].reshape(B, MEM_TOKENS, MEM_WIDTH)
    mem_v = mem_kv[:, MEM_WIDTH:].reshape(B, MEM_TOKENS, MEM_WIDTH)
    cfg_p = dict(tm=1024, C=64, HB=16, o_dtype=BF16, cos64=cos64_p, sin64=sin64_p, tkv=512,
                 nb=1, tq_mem=512, tiles_per_mem=S // 512)
    attend_p = lambda q, c_kv, k_pe, ckb, kpb: _attn_prompt(
        q, ckb, kpb, cosq_p, sinq_p, lw['w_uk'], lw['w_uv'], B, S, 128, 512)
    y_p, c_p, k_p, st_p, sh_p = _decoder_group(
        x_prompt.reshape(B * S, D), B, S, jnp.zeros((B, 1, PROJ_W), F32), None, mem_k, mem_v, lw,
        wg, wu, wd, final_norm_g, cfg_p, attend_p)

    cos64_s, sin64_s, cosq_s, sinq_s = _rope_tables(past_len + jnp.arange(T), MLA_HEADS)
    n_s = DB * T
    cfg_s = dict(tm=n_s, C=T, HB=16, o_dtype=F32, cos64=jnp.tile(cos64_s, (n_s // T, 1)),
                 sin64=jnp.tile(sin64_s, (n_s // T, 1)), tkv=n_s, nb=8, tq_mem=T, tiles_per_mem=1)
    attend_s = lambda q, c_kv, k_pe, ckb, kpb: _attn_sample(
        q, c_kv, k_pe, cosq_s, sinq_s, lw['w_uk'], lw['w_uv'], page_table,
        cache_kv_latent.reshape(n_phys, PAGE, KV_LORA),
        jnp.swapaxes(cache_k_rope.reshape(n_phys, PAGE, ROPE), 1, 2),
        DB, T, min(32, page_table.shape[1]))
    y_s, c_s, k_s, st_s, sh_s = _decoder_group(
        x_sample.reshape(n_s, D), DB, T, _to_layout(state_rwkv_shift[0])[:, None, :],
        state_rwkv.reshape(DB, RWKV_HEADS, HEAD, HEAD),
        cache_mem_k.reshape(DB, MEM_TOKENS, MEM_WIDTH), cache_mem_v.reshape(DB, MEM_TOKENS, MEM_WIDTH),
        lw, wg, wu, wd, final_norm_g, cfg_s, attend_s)

    return (y_p.reshape(B, S, D), y_s.reshape(DB, T, D),
            c_p.reshape(1, B, S, KV_LORA), k_p.reshape(1, B, S, ROPE),
            mem_k.reshape(1, B, MEM_TOKENS, MEM_HEADS, MEM_HEAD_DIM),
            mem_v.reshape(1, B, MEM_TOKENS, MEM_HEADS, MEM_HEAD_DIM),
            st_p[None], sh_p[None],
            c_s.reshape(1, DB, T, KV_LORA), k_s.reshape(1, DB, T, ROPE), st_s[None], sh_s[None])
```
